```python
import math
import jax, jax.numpy as jnp
from jax import lax
import numpy as np

D_MODEL = 1024
BATCH = 8
SEQ = 2048
DEPTH = 4

CHUNK = 64
N_A = DEPTH // 2
N_B = DEPTH - N_A
D_FF = 2816
GMLP_WIDTH = 4 * D_MODEL
GMLP_HALF = GMLP_WIDTH // 2
GMLP_WINDOW = 128
GMLP_GROUPS = 8
GMLP_GROUP_DIM = GMLP_HALF // GMLP_GROUPS
N_HEADS = 16
HEAD_DIM = D_MODEL // N_HEADS
LEFT_CHUNKS = 8
BAND = (LEFT_CHUNKS + 1) * CHUNK
LEFT_PAD = LEFT_CHUNKS * CHUNK
MAX_REL = 4 * CHUNK
N_REL = (CHUNK - 1) + MAX_REL + 1
ALPHA = (2.0 * DEPTH) ** 0.25
BETA = (8.0 * DEPTH) ** -0.25
LN_EPS = 1e-5
N_MOD = 9

kernel_name = "hybrid_gmlp_yoco_chunk_attention_encoder"


def layer_norm(x, g, b):
    xf = x.astype(jnp.float32)
    mu = jnp.mean(xf, axis=-1, keepdims=True)
    var = jnp.mean(jnp.square(xf - mu), axis=-1, keepdims=True)
    return ((xf - mu) * lax.rsqrt(var + LN_EPS)).astype(x.dtype) * g + b


def swiglu(h, w_gu, w_down):
    gu = h @ w_gu
    g, u = jnp.split(gu, 2, axis=-1)
    return (jax.nn.silu(g) * u) @ w_down


def gmlp_mixer(h, w_in, b_in, ln_g, ln_b, w_s, b_s, w_out):
    B, S, _ = h.shape
    z = jax.nn.gelu(h @ w_in + b_in, approximate=False)
    u, v = jnp.split(z, 2, axis=-1)
    v = layer_norm(v, ln_g, ln_b)
    v = v.reshape(B, S // GMLP_WINDOW, GMLP_WINDOW, GMLP_GROUPS, GMLP_GROUP_DIM)
    t = np.arange(GMLP_WINDOW)
    mask = ((t[None, :] // CHUNK) <= (t[:, None] // CHUNK)).astype(np.float32)
    ws = w_s * jnp.asarray(mask, dtype=w_s.dtype)[None]
    s = jnp.einsum('gts,bnsgc->bntgc', ws, v) + b_s.T[None, None, :, :, None]
    return (u * s.reshape(B, S, GMLP_HALF)) @ w_out


def chunk_band_attention(h, w_q, rel_bias, w_o, k_pad, v_pad):
    B, S, _ = h.shape
    n_chunks = S // CHUNK
    q = (h @ w_q).reshape(B, n_chunks, CHUNK, N_HEADS, HEAD_DIM)
    q = jnp.transpose(q, (1, 0, 2, 3, 4))
    t = np.arange(CHUNK)
    r = np.arange(BAND)
    dist = t[:, None] + LEFT_PAD - r[None, :]
    idx = np.clip(dist, -(CHUNK - 1), MAX_REL) + (CHUNK - 1)
    bias = rel_bias[:, idx].astype(jnp.float32)
    scale = HEAD_DIM ** -0.5
    r_j = jnp.arange(BAND, dtype=jnp.int32)

    def one_chunk(args):
        n, qn = args
        start = n * CHUNK
        kn = lax.dynamic_slice_in_dim(k_pad, start, BAND, axis=1)
        vn = lax.dynamic_slice_in_dim(v_pad, start, BAND, axis=1)
        valid = (start - LEFT_PAD + r_j) >= 0
        sc = jnp.einsum('bthd,brhd->bhtr', qn, kn).astype(jnp.float32) * scale + bias
        sc = jnp.where(valid[None, None, None, :], sc, -jnp.inf)
        p = jax.nn.softmax(sc, axis=-1).astype(vn.dtype)
        return jnp.einsum('bhtr,brhd->bthd', p, vn)

    out = lax.map(one_chunk, (jnp.arange(n_chunks, dtype=jnp.int32), q))
    out = jnp.transpose(out, (1, 0, 2, 3, 4)).reshape(B, S, D_MODEL)
    return out @ w_o


def setup_inputs(seed: int = 0) -> dict:
    key = jax.random.key(seed)
    ks = jax.random.split(key, 24)
    D = D_MODEL
    f32 = jnp.float32
    nrm = lambda k, shape: jax.random.normal(k, shape, dtype=f32)
    v_scale = jnp.concatenate([jnp.ones((D,), f32), jnp.full((D,), BETA, f32)])
    return {
        "x": nrm(ks[0], (BATCH, SEQ, D)),
        "c": nrm(ks[1], (BATCH, D)),
        "w_ada": nrm(ks[2], (DEPTH, D, N_MOD * D)) * (0.1 * D ** -0.5),
        "b_ada": nrm(ks[3], (DEPTH, N_MOD * D)) * 0.01,
        "ln_g": 1.0 + 0.01 * nrm(ks[4], (DEPTH, 3, D)),
        "ln_b": 0.01 * nrm(ks[5], (DEPTH, 3, D)),
        "ffn_gu": nrm(ks[6], (DEPTH, 2, D, 2 * D_FF)) * D ** -0.5,
        "ffn_down": nrm(ks[7], (DEPTH, 2, D_FF, D)) * (BETA * D_FF ** -0.5),
        "gmlp_w_in": nrm(ks[8], (N_A, D, GMLP_WIDTH)) * D ** -0.5,
        "gmlp_b_in": 0.01 * nrm(ks[9], (N_A, GMLP_WIDTH)),
        "gmlp_ln_g": 1.0 + 0.01 * nrm(ks[10], (N_A, GMLP_HALF)),
        "gmlp_ln_b": 0.01 * nrm(ks[11], (N_A, GMLP_HALF)),
        "gmlp_w_s": nrm(ks[12], (N_A, GMLP_GROUPS, GMLP_WINDOW, GMLP_WINDOW)) * (0.5 * GMLP_WINDOW ** -0.5),
        "gmlp_b_s": 1.0 + 0.01 * nrm(ks[13], (N_A, GMLP_GROUPS, GMLP_WINDOW)),
        "gmlp_w_out": nrm(ks[14], (N_A, GMLP_HALF, D)) * (BETA * GMLP_HALF ** -0.5),
        "w_ada_kv": nrm(ks[15], (D, 2 * D)) * (0.1 * D ** -0.5),
        "b_ada_kv": 0.01 * nrm(ks[16], (2 * D,)),
        "w_kv": nrm(ks[17], (D, 2 * D)) * D ** -0.5 * v_scale[None, :],
        "attn_w_q": nrm(ks[18], (N_B, D, D)) * D ** -0.5,
        "attn_rel_bias": 0.5 * nrm(ks[19], (N_B, N_HEADS, N_REL)),
        "attn_w_o": nrm(ks[20], (N_B, D, D)) * (BETA * D ** -0.5),
    }


def reference(x, c, w_ada, b_ada, ln_g, ln_b, ffn_gu, ffn_down,
              gmlp_w_in, gmlp_b_in, gmlp_ln_g, gmlp_ln_b, gmlp_w_s, gmlp_b_s, gmlp_w_out,
              w_ada_kv, b_ada_kv, w_kv, attn_w_q, attn_rel_bias, attn_w_o):
    B, S, D = x.shape
    c_act = jax.nn.silu(c)
    k_pad = None
    v_pad = None
    for l in range(DEPTH):
        mod = (c_act @ w_ada[l] + b_ada[l]).reshape(B, 1, N_MOD, D)
        shift = [mod[:, :, 3 * i] for i in range(3)]
        scl = [mod[:, :, 3 * i + 1] for i in range(3)]
        gate = [1.0 + mod[:, :, 3 * i + 2] for i in range(3)]

        h = x * (1.0 + scl[0]) + shift[0]
        y = swiglu(h, ffn_gu[l, 0], ffn_down[l, 0])
        x = layer_norm(ALPHA * x + 0.5 * gate[0] * y, ln_g[l, 0], ln_b[l, 0])

        h = x * (1.0 + scl[1]) + shift[1]
        if l < N_A:
            y = gmlp_mixer(h, gmlp_w_in[l], gmlp_b_in[l], gmlp_ln_g[l], gmlp_ln_b[l],
                           gmlp_w_s[l], gmlp_b_s[l], gmlp_w_out[l])
        else:
            j = l - N_A
            y = chunk_band_attention(h, attn_w_q[j], attn_rel_bias[j], attn_w_o[j], k_pad, v_pad)
        x = layer_norm(ALPHA * x + gate[1] * y, ln_g[l, 1], ln_b[l, 1])

        h = x * (1.0 + scl[2]) + shift[2]
        y = swiglu(h, ffn_gu[l, 1], ffn_down[l, 1])
        x = layer_norm(ALPHA * x + 0.5 * gate[2] * y, ln_g[l, 2], ln_b[l, 2])

        if l == N_A - 1:
            mkv = (c_act @ w_ada_kv + b_ada_kv).reshape(B, 1, 2, D)
            hkv = x * (1.0 + mkv[:, :, 1]) + mkv[:, :, 0]
            kv = hkv @ w_kv
            k = kv[..., :D].reshape(B, S, N_HEADS, HEAD_DIM)
            v = kv[..., D:].reshape(B, S, N_HEADS, HEAD_DIM)
            pad = ((0, 0), (LEFT_PAD, 0), (0, 0), (0, 0))
            k_pad = jnp.pad(k, pad)
            v_pad = jnp.pad(v, pad)
    return x
```

```python
import functools

import jax
import jax.numpy as jnp
from jax import lax
from jax.experimental import pallas as pl
from jax.experimental.pallas import tpu as pltpu

DEPTH = 4
CHUNK = 64
N_A = DEPTH // 2
N_B = DEPTH - N_A
GMLP_WINDOW = 128
GMLP_GROUPS = 8
N_HEADS = 16
LEFT_CHUNKS = 8
LEFT_PAD = LEFT_CHUNKS * CHUNK
MAX_REL = 4 * CHUNK
N_REL = (CHUNK - 1) + MAX_REL + 1
ALPHA = (2.0 * DEPTH) ** 0.25
LN_EPS = 1e-5
N_MOD = 9

V7X_LANES = 128
V7X_MXU_N = 256
V7X_VMEM_LIMIT_BYTES = 56 * 1024 * 1024

NEG_BIG = -1e30

F32 = jnp.float32
BF16 = jnp.bfloat16


def _params(n_axes):
    return pltpu.CompilerParams(
        dimension_semantics=("arbitrary",) * n_axes,
        vmem_limit_bytes=V7X_VMEM_LIMIT_BYTES,
    )


def _resident(block_shape, index_map):
    return pl.BlockSpec(block_shape, index_map, pipeline_mode=pl.Buffered(1))


def _layer_norm(r, g, b):
    mu = jnp.mean(r, axis=-1, keepdims=True)
    d = r - mu
    var = jnp.mean(d * d, axis=-1, keepdims=True)
    return d * lax.rsqrt(var + LN_EPS) * g + b


def _gelu(z):
    return 0.5 * z * (1.0 + lax.erf(z * (0.5 ** 0.5)))


def _modulate(x, mod_ref, sub):
    shift = mod_ref[3 * sub:3 * sub + 1, :]
    scale = mod_ref[3 * sub + 1:3 * sub + 2, :]
    gate = 1.0 + mod_ref[3 * sub + 2:3 * sub + 3, :]
    return x * (1.0 + scale) + shift, gate


def _adaln_body(c_ref, w_ref, b_ref, o_ref):
    c = c_ref[...]
    c_act = c * jax.nn.sigmoid(c)
    o_ref[...] = jnp.dot(c_act, w_ref[...], preferred_element_type=F32) + b_ref[...]


def _adaln(c, w, b, tn):
    n_l, d, n = w.shape
    bsz = c.shape[0]
    return pl.pallas_call(
        _adaln_body,
        grid=(n_l, n // tn),
        in_specs=[
            pl.BlockSpec((bsz, d), lambda l, j: (0, 0)),
            pl.BlockSpec((None, d, tn), lambda l, j: (l, 0, j)),
            pl.BlockSpec((None, 1, tn), lambda l, j: (l, 0, j)),
        ],
        out_specs=pl.BlockSpec((None, bsz, tn), lambda l, j: (l, 0, j)),
        out_shape=jax.ShapeDtypeStruct((n_l, bsz, n), F32),
        compiler_params=_params(2),
        name="adaln",
    )(c, w, b.reshape(n_l, 1, n))


def _ffn_body(sub, d_ff, x_ref, mod_ref, wgu_ref, wd_ref, lng_ref, lnb_ref, o_ref, a_ref):
    x = x_ref[...]
    h, gate = _modulate(x, mod_ref, sub)
    h = h.astype(BF16)
    for j in range(d_ff // V7X_MXU_N):
        lo = j * V7X_MXU_N
        g = jnp.dot(h, wgu_ref[:, lo:lo + V7X_MXU_N], preferred_element_type=F32)
        u = jnp.dot(h, wgu_ref[:, d_ff + lo:d_ff + lo + V7X_MXU_N], preferred_element_type=F32)
        a_ref[:, lo:lo + V7X_MXU_N] = (g * jax.nn.sigmoid(g) * u).astype(BF16)
    y = jnp.dot(a_ref[...], wd_ref[...], preferred_element_type=F32)
    r = ALPHA * x + (0.5 * gate) * y
    o_ref[...] = _layer_norm(r, lng_ref[sub:sub + 1, :], lnb_ref[sub:sub + 1, :])


def _ffn(x, mod, w_gu, w_down, ln_g, ln_b, *, layer, sub, half, seq, tm):
    m, d = x.shape
    d_ff = w_down.shape[2]
    tiles_per_seq = seq // tm
    return pl.pallas_call(
        functools.partial(_ffn_body, sub, d_ff),
        grid=(m // tm,),
        in_specs=[
            pl.BlockSpec((tm, d), lambda i: (i, 0)),
            pl.BlockSpec((None, None, N_MOD, d), lambda i: (layer, i // tiles_per_seq, 0, 0)),
            _resident((None, None, d, 2 * d_ff), lambda i: (layer, half, 0, 0)),
            _resident((None, None, d_ff, d), lambda i: (layer, half, 0, 0)),
            _resident((None, 3, d), lambda i: (layer, 0, 0)),
            _resident((None, 3, d), lambda i: (layer, 0, 0)),
        ],
        out_specs=pl.BlockSpec((tm, d), lambda i: (i, 0)),
        out_shape=jax.ShapeDtypeStruct((m, d), F32),
        scratch_shapes=[pltpu.VMEM((tm, d_ff), BF16)],
        compiler_params=_params(1),
        name="ffn",
    )(x, mod, w_gu, w_down, ln_g, ln_b)


def _gmlp_body(x_ref, mod_ref, win_ref, bin_ref, glng_ref, glnb_ref, ws_ref, bst_ref, wout_ref,
               lng_ref, lnb_ref, o_ref, v_ref, m_ref):
    tm = x_ref.shape[0]
    half = wout_ref.shape[0]
    gdim = half // GMLP_GROUPS
    x = x_ref[...]
    h, gate = _modulate(x, mod_ref, 1)
    h = h.astype(BF16)

    for g in range(GMLP_GROUPS):
        lo = half + g * gdim
        z = jnp.dot(h, win_ref[:, lo:lo + gdim], preferred_element_type=F32) + bin_ref[:, lo:lo + gdim]
        v_ref[:, g * gdim:(g + 1) * gdim] = _gelu(z)
    v = v_ref[...]
    mu = jnp.mean(v, axis=-1, keepdims=True)
    dv = v - mu
    rstd = lax.rsqrt(jnp.mean(dv * dv, axis=-1, keepdims=True) + LN_EPS)

    t_out = lax.broadcasted_iota(jnp.int32, (GMLP_WINDOW, GMLP_WINDOW), 0)
    s_in = lax.broadcasted_iota(jnp.int32, (GMLP_WINDOW, GMLP_WINDOW), 1)
    causal = (s_in // CHUNK) <= (t_out // CHUNK)

    for g in range(GMLP_GROUPS):
        lo = g * gdim
        vn = ((v_ref[:, lo:lo + gdim] - mu) * rstd * glng_ref[:, lo:lo + gdim]
              + glnb_ref[:, lo:lo + gdim]).astype(BF16)
        ws_g = jnp.where(causal, ws_ref[g], 0.0).astype(BF16)
        b_col = bst_ref[:, g:g + 1]
        u = _gelu(jnp.dot(h, win_ref[:, lo:lo + gdim], preferred_element_type=F32)
                  + bin_ref[:, lo:lo + gdim])
        for w in range(tm // GMLP_WINDOW):
            rows = slice(w * GMLP_WINDOW, (w + 1) * GMLP_WINDOW)
            s = jnp.dot(ws_g, vn[rows], preferred_element_type=F32) + b_col
            m_ref[rows, lo:lo + gdim] = (u[rows] * s).astype(BF16)

    y = jnp.dot(m_ref[...], wout_ref[...], preferred_element_type=F32)
    r = ALPHA * x + gate * y
    o_ref[...] = _layer_norm(r, lng_ref[1:2, :], lnb_ref[1:2, :])


def _gmlp(x, mod, w_in, b_in, gln_g, gln_b, w_s, b_s_t, w_out, ln_g, ln_b, *, layer, seq, tm):
    m, d = x.shape
    width = w_in.shape[2]
    half = width // 2
    tiles_per_seq = seq // tm
    const = lambda i: (layer, 0, 0)
    return pl.pallas_call(
        _gmlp_body,
        grid=(m // tm,),
        in_specs=[
            pl.BlockSpec((tm, d), lambda i: (i, 0)),
            pl.BlockSpec((None, None, N_MOD, d), lambda i: (layer, i // tiles_per_seq, 0, 0)),
            _resident((None, d, width), const),
            _resident((None, 1, width), const),
            _resident((None, 1, half), const),
            _resident((None, 1, half), const),
            _resident((None, GMLP_GROUPS, GMLP_WINDOW, GMLP_WINDOW), lambda i: (layer, 0, 0, 0)),
            _resident((None, GMLP_WINDOW, GMLP_GROUPS), const),
            _resident((None, half, d), const),
            _resident((None, 3, d), const),
            _resident((None, 3, d), const),
        ],
        out_specs=pl.BlockSpec((tm, d), lambda i: (i, 0)),
        out_shape=jax.ShapeDtypeStruct((m, d), F32),
        scratch_shapes=[pltpu.VMEM((tm, half), F32), pltpu.VMEM((tm, half), BF16)],
        compiler_params=_params(1),
        name="gmlp",
    )(x, mod, w_in, b_in, gln_g, gln_b, w_s, b_s_t, w_out, ln_g, ln_b)


def _kv_body(x_ref, mkv_ref, wkv_ref, k_ref, v_ref):
    d = x_ref.shape[1]
    i = pl.program_id(1)

    @pl.when(i == 0)
    def _():
        k_ref[...] = jnp.zeros_like(k_ref)
        v_ref[...] = jnp.zeros_like(v_ref)

    @pl.when(i > 0)
    def _():
        h = (x_ref[...] * (1.0 + mkv_ref[1:2, :]) + mkv_ref[0:1, :]).astype(BF16)
        kv = jnp.dot(h, wkv_ref[...], preferred_element_type=F32)
        k_ref[...] = kv[:, :d].astype(BF16)
        v_ref[...] = kv[:, d:].astype(BF16)


def _kv_project(x, mkv, w_kv):
    bsz, seq, d = x.shape
    tm = LEFT_PAD
    out = jax.ShapeDtypeStruct((bsz, LEFT_PAD + seq, d), BF16)
    return pl.pallas_call(
        _kv_body,
        grid=(bsz, 1 + seq // tm),
        in_specs=[
            pl.BlockSpec((None, tm, d), lambda b, i: (b, jnp.maximum(i - 1, 0), 0)),
            pl.BlockSpec((None, 2, d), lambda b, i: (b, 0, 0)),
            _resident((d, 2 * d), lambda b, i: (0, 0)),
        ],
        out_specs=[pl.BlockSpec((None, tm, d), lambda b, i: (b, i, 0))] * 2,
        out_shape=[out, out],
        compiler_params=_params(2),
        name="kv_project",
    )(x, mkv, w_kv)


def _bias_body(tq, tk, rb_ref, o_ref):
    rb = rb_ref[...]
    hi = rb.astype(BF16)
    rem = rb - hi.astype(F32)
    mid = rem.astype(BF16)
    lo = (rem - mid.astype(F32)).astype(BF16)
    r = lax.broadcasted_iota(jnp.int32, (1, tk), 1)
    rel = lax.broadcasted_iota(jnp.int32, (N_REL, tk), 0)
    rows = o_ref.shape[1]
    for tt in range(rows):
        t = pl.program_id(0) * rows + tt
        idx = jnp.clip(t + LEFT_PAD - r, -(CHUNK - 1), MAX_REL) + (CHUNK - 1)
        onehot = (rel == idx).astype(BF16)
        val = (jnp.dot(hi, onehot, preferred_element_type=F32)
               + jnp.dot(mid, onehot, preferred_element_type=F32)
               + jnp.dot(lo, onehot, preferred_element_type=F32))
        qc = t // CHUNK
        kc = r // CHUNK
        visible = (kc >= qc) & (kc <= qc + LEFT_CHUNKS)
        o_ref[:, tt, :] = jnp.where(visible, val, NEG_BIG)


def _bias_table(rel_bias, tq):
    nh = rel_bias.shape[0]
    tk = tq + LEFT_PAD
    rows = 8
    return pl.pallas_call(
        functools.partial(_bias_body, tq, tk),
        grid=(tq // rows,),
        in_specs=[pl.BlockSpec((nh, N_REL), lambda i: (0, 0))],
        out_specs=pl.BlockSpec((nh, rows, tk), lambda i: (0, i, 0)),
        out_shape=jax.ShapeDtypeStruct((nh, tq, tk), F32),
        compiler_params=_params(1),
        name="bias_table",
    )(rel_bias)


def _attn_body(x_ref, mod_ref, wq_ref, wo_ref, k_ref, v_ref, bias_ref, lng_ref, lnb_ref, o_ref, ctx_ref):
    tq, d = x_ref.shape
    tk = tq + LEFT_PAD
    hd = d // N_HEADS
    x = x_ref[...]
    h, gate = _modulate(x, mod_ref, 1)
    q = jnp.dot(h.astype(BF16), wq_ref[...], preferred_element_type=F32) * (hd ** -0.5)
    q = q.astype(BF16)

    start = pl.multiple_of(pl.program_id(1) * tq, tq)
    col = lax.broadcasted_iota(jnp.int32, (1, tk), 1)
    not_padding = col >= LEFT_PAD - start
    lane = lax.broadcasted_iota(jnp.int32, (tq, V7X_LANES), 1)
    first_head = lane < hd

    for p in range(d // V7X_LANES):
        lanes = slice(p * V7X_LANES, (p + 1) * V7X_LANES)
        q_pair = q[:, lanes]
        k_pair = k_ref[pl.ds(start, tk), lanes]
        v_pair = v_ref[pl.ds(start, tk), lanes]
        outs = []
        for e in range(2):
            q_one = jnp.where(first_head if e == 0 else ~first_head, q_pair, jnp.zeros_like(q_pair))
            s = lax.dot_general(q_one, k_pair, (((1,), (1,)), ((), ())), preferred_element_type=F32)
            s = jnp.where(not_padding, s + bias_ref[2 * p + e], NEG_BIG)
            s_max = jnp.max(s, axis=-1, keepdims=True)
            w = jnp.exp(s - s_max)
            denom = jnp.sum(w, axis=-1, keepdims=True)
            ctx = jnp.dot(w.astype(BF16), v_pair, preferred_element_type=F32)
            outs.append(ctx / denom)
        ctx_ref[:, lanes] = jnp.where(first_head, outs[0], outs[1]).astype(BF16)

    y = jnp.dot(ctx_ref[...], wo_ref[...], preferred_element_type=F32)
    r = ALPHA * x + gate * y
    o_ref[...] = _layer_norm(r, lng_ref[1:2, :], lnb_ref[1:2, :])


def _attention(x, mod, w_q, w_o, k_pad, v_pad, bias, ln_g, ln_b, *, layer, j, tq):
    bsz, seq, d = x.shape
    kv_rows = k_pad.shape[1]
    tk = tq + LEFT_PAD
    return pl.pallas_call(
        _attn_body,
        grid=(bsz, seq // tq),
        in_specs=[
            pl.BlockSpec((None, tq, d), lambda b, i: (b, i, 0)),
            pl.BlockSpec((None, None, N_MOD, d), lambda b, i: (layer, b, 0, 0)),
            _resident((None, d, d), lambda b, i: (j, 0, 0)),
            _resident((None, d, d), lambda b, i: (j, 0, 0)),
            pl.BlockSpec((None, kv_rows, d), lambda b, i: (b, 0, 0)),
            pl.BlockSpec((None, kv_rows, d), lambda b, i: (b, 0, 0)),
            _resident((N_HEADS, tq, tk), lambda b, i: (j, 0, 0)),
            _resident((None, 3, d), lambda b, i: (layer, 0, 0)),
            _resident((None, 3, d), lambda b, i: (layer, 0, 0)),
        ],
        out_specs=pl.BlockSpec((None, tq, d), lambda b, i: (b, i, 0)),
        out_shape=jax.ShapeDtypeStruct((bsz, seq, d), F32),
        scratch_shapes=[pltpu.VMEM((tq, d), BF16)],
        compiler_params=_params(2),
        name="attention",
    )(x, mod, w_q, w_o, k_pad, v_pad, bias, ln_g, ln_b)


FFN_TM = 512
GMLP_TM = 256
ATTN_TQ = 128
ADALN_TN = 1024


def kernel(x, c, w_ada, b_ada, ln_g, ln_b, ffn_gu, ffn_down, gmlp_w_in, gmlp_b_in, gmlp_ln_g, gmlp_ln_b,
           gmlp_w_s, gmlp_b_s, gmlp_w_out, w_ada_kv, b_ada_kv, w_kv, attn_w_q, attn_rel_bias, attn_w_o):
    bsz, seq, d = x.shape
    mod = _adaln(c, w_ada, b_ada, ADALN_TN).reshape(DEPTH, bsz, N_MOD, d)
    mkv = _adaln(c, w_ada_kv[None], b_ada_kv[None], ADALN_TN).reshape(bsz, 2, d)

    ffn_gu_b = ffn_gu.astype(BF16)
    ffn_down_b = ffn_down.astype(BF16)
    w_in_b = gmlp_w_in.astype(BF16)
    w_out_b = gmlp_w_out.astype(BF16)
    w_kv_b = w_kv.astype(BF16)
    w_q_b = attn_w_q.astype(BF16)
    w_o_b = attn_w_o.astype(BF16)
    b_in = gmlp_b_in[:, None, :]
    gln_g = gmlp_ln_g[:, None, :]
    gln_b = gmlp_ln_b[:, None, :]
    b_s_t = jnp.swapaxes(gmlp_b_s, 1, 2)
    bias = _bias_table(attn_rel_bias.reshape(N_B * N_HEADS, N_REL), ATTN_TQ)

    xf = x.reshape(bsz * seq, d)
    k_pad = v_pad = None
    for l in range(DEPTH):
        ffn = functools.partial(_ffn, mod=mod, w_gu=ffn_gu_b, w_down=ffn_down_b, ln_g=ln_g, ln_b=ln_b,
                                layer=l, seq=seq, tm=FFN_TM)
        xf = ffn(xf, sub=0, half=0)
        if l < N_A:
            xf = _gmlp(xf, mod, w_in_b, b_in, gln_g, gln_b, gmlp_w_s, b_s_t, w_out_b, ln_g, ln_b,
                       layer=l, seq=seq, tm=GMLP_TM)
        else:
            xf = _attention(xf.reshape(bsz, seq, d), mod, w_q_b, w_o_b, k_pad, v_pad, bias, ln_g, ln_b,
                            layer=l, j=l - N_A, tq=ATTN_TQ).reshape(bsz * seq, d)
        xf = ffn(xf, sub=2, half=1)
        if l == N_A - 1:
            k_pad, v_pad = _kv_project(xf.reshape(bsz, seq, d), mkv, w_kv_b)
    return xf.reshape(bsz, seq, d)
```

```python
import functools

import jax
import jax.numpy as jnp
from jax import lax
from jax.experimental import pallas as pl
from jax.experimental.pallas import tpu as pltpu

DEPTH = 4
CHUNK = 64
N_A = DEPTH // 2
N_B = DEPTH - N_A
GMLP_WINDOW = 128
GMLP_GROUPS = 8
N_HEADS = 16
LEFT_CHUNKS = 8
LEFT_PAD = LEFT_CHUNKS * CHUNK
MAX_REL = 4 * CHUNK
N_REL = (CHUNK - 1) + MAX_REL + 1
ALPHA = (2.0 * DEPTH) ** 0.25
LN_EPS = 1e-5
N_MOD = 9

V7X_LANES = 128
V7X_MXU_N = 256
V7X_BF16_SUBLANES = 16
V7X_VMEM_LIMIT_BYTES = 56 * 1024 * 1024

NEG_BIG = -1e30
LOG2E = 1.4426950408889634

KV_BLOCK = 256
ATTN_SUB_Q = 128
ATTN_HEADS_AHEAD = 3

F32 = jnp.float32
BF16 = jnp.bfloat16


def _params(n_axes):
    return pltpu.CompilerParams(
        dimension_semantics=("arbitrary",) * n_axes,
        vmem_limit_bytes=V7X_VMEM_LIMIT_BYTES,
    )


def _resident(block_shape, index_map):
    return pl.BlockSpec(block_shape, index_map, pipeline_mode=pl.Buffered(1))


def _layer_norm(r, g, b):
    mu = jnp.mean(r, axis=-1, keepdims=True)
    d = r - mu
    var = jnp.mean(d * d, axis=-1, keepdims=True)
    return d * lax.rsqrt(var + LN_EPS) * g + b


def _gelu(z):
    return 0.5 * z * (1.0 + lax.erf(z * (0.5 ** 0.5)))


def _modulate(x, mod_ref, sub):
    shift = mod_ref[3 * sub:3 * sub + 1, :]
    scale = mod_ref[3 * sub + 1:3 * sub + 2, :]
    gate = 1.0 + mod_ref[3 * sub + 2:3 * sub + 3, :]
    return x * (1.0 + scale) + shift, gate


def _adaln_body(c_ref, w_ref, b_ref, o_ref):
    c = c_ref[...]
    c_act = c * jax.nn.sigmoid(c)
    o_ref[...] = jnp.dot(c_act, w_ref[...], preferred_element_type=F32) + b_ref[...]


def _adaln(c, w, b, tn):
    n_l, d, n = w.shape
    bsz = c.shape[0]
    return pl.pallas_call(
        _adaln_body,
        grid=(n_l, n // tn),
        in_specs=[
            pl.BlockSpec((bsz, d), lambda l, j: (0, 0)),
            pl.BlockSpec((None, d, tn), lambda l, j: (l, 0, j)),
            pl.BlockSpec((None, 1, tn), lambda l, j: (l, 0, j)),
        ],
        out_specs=pl.BlockSpec((None, bsz, tn), lambda l, j: (l, 0, j)),
        out_shape=jax.ShapeDtypeStruct((n_l, bsz, n), F32),
        compiler_params=_params(2),
        name="adaln",
    )(c, w, b.reshape(n_l, 1, n))


def _ffn_body(sub, d_ff, x_ref, mod_ref, wgu_ref, wd_ref, lng_ref, lnb_ref, o_ref, a_ref):
    x = x_ref[...]
    h, gate = _modulate(x, mod_ref, sub)
    h = h.astype(BF16)
    for j in range(d_ff // V7X_MXU_N):
        lo = j * V7X_MXU_N
        g = jnp.dot(h, wgu_ref[:, lo:lo + V7X_MXU_N], preferred_element_type=F32)
        u = jnp.dot(h, wgu_ref[:, d_ff + lo:d_ff + lo + V7X_MXU_N], preferred_element_type=F32)
        a_ref[:, lo:lo + V7X_MXU_N] = (g * jax.nn.sigmoid(g) * u).astype(BF16)
    y = jnp.dot(a_ref[...], wd_ref[...], preferred_element_type=F32)
    r = ALPHA * x + (0.5 * gate) * y
    o_ref[...] = _layer_norm(r, lng_ref[sub:sub + 1, :], lnb_ref[sub:sub + 1, :])


def _ffn(x, mod, w_gu, w_down, ln_g, ln_b, *, layer, sub, half, seq, tm):
    m, d = x.shape
    d_ff = w_down.shape[2]
    tiles_per_seq = seq // tm
    return pl.pallas_call(
        functools.partial(_ffn_body, sub, d_ff),
        grid=(m // tm,),
        in_specs=[
            pl.BlockSpec((tm, d), lambda i: (i, 0)),
            pl.BlockSpec((None, None, N_MOD, d), lambda i: (layer, i // tiles_per_seq, 0, 0)),
            _resident((None, None, d, 2 * d_ff), lambda i: (layer, half, 0, 0)),
            _resident((None, None, d_ff, d), lambda i: (layer, half, 0, 0)),
            _resident((None, 3, d), lambda i: (layer, 0, 0)),
            _resident((None, 3, d), lambda i: (layer, 0, 0)),
        ],
        out_specs=pl.BlockSpec((tm, d), lambda i: (i, 0)),
        out_shape=jax.ShapeDtypeStruct((m, d), F32),
        scratch_shapes=[pltpu.VMEM((tm, d_ff), BF16)],
        compiler_params=_params(1),
        name="ffn",
    )(x, mod, w_gu, w_down, ln_g, ln_b)


def _gmlp_body(x_ref, mod_ref, win_ref, bin_ref, glng_ref, glnb_ref, ws_ref, bst_ref, wout_ref,
               lng_ref, lnb_ref, o_ref, v_ref, m_ref):
    tm = x_ref.shape[0]
    half = wout_ref.shape[0]
    gdim = half // GMLP_GROUPS
    x = x_ref[...]
    h, gate = _modulate(x, mod_ref, 1)
    h = h.astype(BF16)

    for g in range(GMLP_GROUPS):
        lo = half + g * gdim
        z = jnp.dot(h, win_ref[:, lo:lo + gdim], preferred_element_type=F32) + bin_ref[:, lo:lo + gdim]
        v_ref[:, g * gdim:(g + 1) * gdim] = _gelu(z)
    v = v_ref[...]
    mu = jnp.mean(v, axis=-1, keepdims=True)
    dv = v - mu
    rstd = lax.rsqrt(jnp.mean(dv * dv, axis=-1, keepdims=True) + LN_EPS)

    t_out = lax.broadcasted_iota(jnp.int32, (GMLP_WINDOW, GMLP_WINDOW), 0)
    s_in = lax.broadcasted_iota(jnp.int32, (GMLP_WINDOW, GMLP_WINDOW), 1)
    causal = (s_in // CHUNK) <= (t_out // CHUNK)

    for g in range(GMLP_GROUPS):
        lo = g * gdim
        vn = ((v_ref[:, lo:lo + gdim] - mu) * rstd * glng_ref[:, lo:lo + gdim]
              + glnb_ref[:, lo:lo + gdim]).astype(BF16)
        ws_g = jnp.where(causal, ws_ref[g], 0.0).astype(BF16)
        b_col = bst_ref[:, g:g + 1]
        u = _gelu(jnp.dot(h, win_ref[:, lo:lo + gdim], preferred_element_type=F32)
                  + bin_ref[:, lo:lo + gdim])
        for w in range(tm // GMLP_WINDOW):
            rows = slice(w * GMLP_WINDOW, (w + 1) * GMLP_WINDOW)
            s = jnp.dot(ws_g, vn[rows], preferred_element_type=F32) + b_col
            m_ref[rows, lo:lo + gdim] = (u[rows] * s).astype(BF16)

    y = jnp.dot(m_ref[...], wout_ref[...], preferred_element_type=F32)
    r = ALPHA * x + gate * y
    o_ref[...] = _layer_norm(r, lng_ref[1:2, :], lnb_ref[1:2, :])


def _gmlp(x, mod, w_in, b_in, gln_g, gln_b, w_s, b_s_t, w_out, ln_g, ln_b, *, layer, seq, tm):
    m, d = x.shape
    width = w_in.shape[2]
    half = width // 2
    tiles_per_seq = seq // tm
    const = lambda i: (layer, 0, 0)
    return pl.pallas_call(
        _gmlp_body,
        grid=(m // tm,),
        in_specs=[
            pl.BlockSpec((tm, d), lambda i: (i, 0)),
            pl.BlockSpec((None, None, N_MOD, d), lambda i: (layer, i // tiles_per_seq, 0, 0)),
            _resident((None, d, width), const),
            _resident((None, 1, width), const),
            _resident((None, 1, half), const),
            _resident((None, 1, half), const),
            _resident((None, GMLP_GROUPS, GMLP_WINDOW, GMLP_WINDOW), lambda i: (layer, 0, 0, 0)),
            _resident((None, GMLP_WINDOW, GMLP_GROUPS), const),
            _resident((None, half, d), const),
            _resident((None, 3, d), const),
            _resident((None, 3, d), const),
        ],
        out_specs=pl.BlockSpec((tm, d), lambda i: (i, 0)),
        out_shape=jax.ShapeDtypeStruct((m, d), F32),
        scratch_shapes=[pltpu.VMEM((tm, half), F32), pltpu.VMEM((tm, half), BF16)],
        compiler_params=_params(1),
        name="gmlp",
    )(x, mod, w_in, b_in, gln_g, gln_b, w_s, b_s_t, w_out, ln_g, ln_b)


def _kv_body(x_ref, mkv_ref, wk_ref, wvt_ref, k_ref, vt_ref):
    i = pl.program_id(1)

    @pl.when(i == 0)
    def _():
        k_ref[...] = jnp.zeros_like(k_ref)
        vt_ref[...] = jnp.zeros_like(vt_ref)

    @pl.when(i > 0)
    def _():
        h = (x_ref[...] * (1.0 + mkv_ref[1:2, :]) + mkv_ref[0:1, :]).astype(BF16)
        k = jnp.dot(h, wk_ref[...], preferred_element_type=F32)
        k_ref[...] = k.astype(BF16).reshape(k_ref.shape)
        vt = lax.dot_general(wvt_ref[...], h, (((1,), (1,)), ((), ())), preferred_element_type=F32)
        for n in range(vt_ref.shape[0]):
            vt_ref[n] = vt[:, n * KV_BLOCK:(n + 1) * KV_BLOCK].astype(BF16)


def _kv_project(x, mkv, w_k, w_v_t):
    bsz, seq, d = x.shape
    tm = LEFT_PAD
    blocks_per_tile = tm // KV_BLOCK
    n_blocks = (LEFT_PAD + seq) // KV_BLOCK
    return pl.pallas_call(
        _kv_body,
        grid=(bsz, 1 + seq // tm),
        in_specs=[
            pl.BlockSpec((None, tm, d), lambda b, i: (b, jnp.maximum(i - 1, 0), 0)),
            pl.BlockSpec((None, 2, d), lambda b, i: (b, 0, 0)),
            _resident((d, d), lambda b, i: (0, 0)),
            _resident((d, d), lambda b, i: (0, 0)),
        ],
        out_specs=[
            pl.BlockSpec((None, blocks_per_tile, KV_BLOCK, d), lambda b, i: (b, i, 0, 0)),
            pl.BlockSpec((None, blocks_per_tile, d, KV_BLOCK), lambda b, i: (b, i, 0, 0)),
        ],
        out_shape=[
            jax.ShapeDtypeStruct((bsz, n_blocks, KV_BLOCK, d), BF16),
            jax.ShapeDtypeStruct((bsz, n_blocks, d, KV_BLOCK), BF16),
        ],
        compiler_params=_params(2),
        name="kv_project",
    )(x, mkv, w_k, w_v_t)


def _bias_body(rb_ref, o_ref):
    tq = o_ref.shape[2]
    rb = rb_ref[...]
    hi = rb.astype(BF16)
    rem = rb - hi.astype(F32)
    mid = rem.astype(BF16)
    lo = (rem - mid.astype(F32)).astype(BF16)
    t = lax.broadcasted_iota(jnp.int32, (1, tq), 1)
    rel = lax.broadcasted_iota(jnp.int32, (N_REL, tq), 0)
    rows = o_ref.shape[1]
    for rr in range(rows):
        r = pl.program_id(0) * rows + rr
        idx = jnp.clip(t + LEFT_PAD - r, -(CHUNK - 1), MAX_REL) + (CHUNK - 1)
        onehot = (rel == idx).astype(BF16)
        val = (jnp.dot(hi, onehot, preferred_element_type=F32)
               + jnp.dot(mid, onehot, preferred_element_type=F32)
               + jnp.dot(lo, onehot, preferred_element_type=F32))
        qc = t // CHUNK
        kc = r // CHUNK
        visible = (kc >= qc) & (kc <= qc + LEFT_CHUNKS)
        o_ref[:, rr, :] = jnp.where(visible, val * LOG2E, NEG_BIG)


def _bias_table(rel_bias, tq):
    nh = rel_bias.shape[0]
    tk = tq + LEFT_PAD
    rows = 8
    return pl.pallas_call(
        _bias_body,
        grid=(tk // rows,),
        in_specs=[pl.BlockSpec((nh, N_REL), lambda i: (0, 0))],
        out_specs=pl.BlockSpec((nh, rows, tq), lambda i: (0, i, 0)),
        out_shape=jax.ShapeDtypeStruct((nh, tk, tq), F32),
        compiler_params=_params(1),
        name="bias_table",
    )(rel_bias)


def _attn_step(mask_padding, x_ref, mod_ref, wq_ref, wo_ref, k_ref, vt_ref, bias_ref, lng_ref, lnb_ref,
               o_ref, ctxt_ref):
    tq, d = x_ref.shape
    win_blocks = (tq + LEFT_PAD) // KV_BLOCK
    tk = win_blocks * KV_BLOCK
    sub_k = ATTN_SUB_Q + LEFT_PAD
    hd = d // N_HEADS
    i = pl.program_id(1)
    x = x_ref[...]
    h, gate = _modulate(x, mod_ref, 1)
    q = jnp.dot(h.astype(BF16), wq_ref[...], preferred_element_type=F32) * (hd ** -0.5 * LOG2E)
    q = q.astype(BF16)

    first_key_row = LEFT_PAD - i * tq
    first_head = lax.broadcasted_iota(jnp.int32, (tq, V7X_LANES), 1) < hd
    ones_rows = jnp.ones((V7X_BF16_SUBLANES, KV_BLOCK), BF16)

    def scores_t(head):
        lanes = slice(head // 2 * V7X_LANES, (head // 2 + 1) * V7X_LANES)
        q_pair = q[:, lanes]
        k_win = k_ref[pl.ds(i, win_blocks), :, lanes].reshape(tk, V7X_LANES)
        q_one = jnp.where(first_head if head % 2 == 0 else ~first_head, q_pair, jnp.zeros_like(q_pair))
        return lax.dot_general(k_win, q_one, (((1,), (1,)), ((), ())), preferred_element_type=F32)

    def softmax_t(head, s_t):
        probs = []
        for r0 in range(0, tq, ATTN_SUB_Q):
            s_sub = s_t[r0:r0 + sub_k, r0:r0 + ATTN_SUB_Q] + bias_ref[head]
            if mask_padding:
                row = lax.broadcasted_iota(jnp.int32, (sub_k, 1), 0)
                s_sub = jnp.where(row >= first_key_row - r0, s_sub, NEG_BIG)
            s_max = jnp.max(s_sub, axis=0, keepdims=True)
            pieces = [jnp.exp2(s_sub - s_max).astype(BF16)]
            if r0:
                pieces.insert(0, jnp.zeros((r0, ATTN_SUB_Q), BF16))
            if tk - sub_k - r0:
                pieces.append(jnp.zeros((tk - sub_k - r0, ATTN_SUB_Q), BF16))
            probs.append(jnp.concatenate(pieces, axis=0))
        return jnp.concatenate(probs, axis=1)

    def context_t(head, p_t):
        rows = slice(head * hd, (head + 1) * hd)
        acc = None
        for n in range(win_blocks):
            v_t = vt_ref[pl.ds(i + n, 1), rows, :].reshape(hd, KV_BLOCK)
            lhs = jnp.concatenate([v_t, ones_rows], axis=0)
            part = jnp.dot(lhs, p_t[n * KV_BLOCK:(n + 1) * KV_BLOCK], preferred_element_type=F32)
            acc = part if acc is None else acc + part
        ctxt_ref[rows, :] = (acc[:hd] / acc[hd:hd + 1]).astype(BF16)

    pending = [scores_t(head) for head in range(ATTN_HEADS_AHEAD)]
    for head in range(N_HEADS):
        s_cur = pending.pop(0)
        if head + ATTN_HEADS_AHEAD < N_HEADS:
            pending.append(scores_t(head + ATTN_HEADS_AHEAD))
        context_t(head, softmax_t(head, s_cur))

    y = lax.dot_general(ctxt_ref[...], wo_ref[...], (((0,), (0,)), ((), ())), preferred_element_type=F32)
    r = ALPHA * x + gate * y
    o_ref[...] = _layer_norm(r, lng_ref[1:2, :], lnb_ref[1:2, :])


def _attn_body(x_ref, *refs):
    has_padding = pl.program_id(1) * x_ref.shape[0] < LEFT_PAD
    pl.when(has_padding)(functools.partial(_attn_step, True, x_ref, *refs))
    pl.when(jnp.logical_not(has_padding))(functools.partial(_attn_step, False, x_ref, *refs))


def _attention(x, mod, w_q, w_o, k_pad, vt_pad, bias, ln_g, ln_b, *, layer, j):
    bsz, seq, d = x.shape
    tq = KV_BLOCK
    kv_blocks = k_pad.shape[1]
    return pl.pallas_call(
        _attn_body,
        grid=(bsz, seq // tq),
        in_specs=[
            pl.BlockSpec((None, tq, d), lambda b, i: (b, i, 0)),
            pl.BlockSpec((None, None, N_MOD, d), lambda b, i: (layer, b, 0, 0)),
            _resident((None, d, d), lambda b, i: (j, 0, 0)),
            _resident((None, d, d), lambda b, i: (j, 0, 0)),
            pl.BlockSpec((None, kv_blocks, KV_BLOCK, d), lambda b, i: (b, 0, 0, 0)),
            pl.BlockSpec((None, kv_blocks, d, KV_BLOCK), lambda b, i: (b, 0, 0, 0)),
            _resident((N_HEADS, ATTN_SUB_Q + LEFT_PAD, ATTN_SUB_Q), lambda b, i: (j, 0, 0)),
            _resident((None, 3, d), lambda b, i: (layer, 0, 0)),
            _resident((None, 3, d), lambda b, i: (layer, 0, 0)),
        ],
        out_specs=pl.BlockSpec((None, tq, d), lambda b, i: (b, i, 0)),
        out_shape=jax.ShapeDtypeStruct((bsz, seq, d), F32),
        scratch_shapes=[pltpu.VMEM((d, tq), BF16)],
        compiler_params=_params(2),
        name="attention",
    )(x, mod, w_q, w_o, k_pad, vt_pad, bias, ln_g, ln_b)


FFN_TM = 512
GMLP_TM = 256
ADALN_TN = 1024


def kernel(x, c, w_ada, b_ada, ln_g, ln_b, ffn_gu, ffn_down, gmlp_w_in, gmlp_b_in, gmlp_ln_g, gmlp_ln_b,
           gmlp_w_s, gmlp_b_s, gmlp_w_out, w_ada_kv, b_ada_kv, w_kv, attn_w_q, attn_rel_bias, attn_w_o):
    bsz, seq, d = x.shape
    mod = _adaln(c, w_ada, b_ada, ADALN_TN).reshape(DEPTH, bsz, N_MOD, d)
    mkv = _adaln(c, w_ada_kv[None], b_ada_kv[None], ADALN_TN).reshape(bsz, 2, d)

    ffn_gu_b = ffn_gu.astype(BF16)
    ffn_down_b = ffn_down.astype(BF16)
    w_in_b = gmlp_w_in.astype(BF16)
    w_out_b = gmlp_w_out.astype(BF16)
    w_k_b = w_kv[:, :d].astype(BF16)
    w_v_t_b = w_kv[:, d:].T.astype(BF16)
    w_q_b = attn_w_q.astype(BF16)
    w_o_b = attn_w_o.astype(BF16)
    b_in = gmlp_b_in[:, None, :]
    gln_g = gmlp_ln_g[:, None, :]
    gln_b = gmlp_ln_b[:, None, :]
    b_s_t = jnp.swapaxes(gmlp_b_s, 1, 2)
    bias = _bias_table(attn_rel_bias.reshape(N_B * N_HEADS, N_REL), ATTN_SUB_Q)

    xf = x.reshape(bsz * seq, d)
    k_pad = vt_pad = None
    for l in range(DEPTH):
        ffn = functools.partial(_ffn, mod=mod, w_gu=ffn_gu_b, w_down=ffn_down_b, ln_g=ln_g, ln_b=ln_b,
                                layer=l, seq=seq, tm=FFN_TM)
        xf = ffn(xf, sub=0, half=0)
        if l < N_A:
            xf = _gmlp(xf, mod, w_in_b, b_in, gln_g, gln_b, gmlp_w_s, b_s_t, w_out_b, ln_g, ln_b,
                       layer=l, seq=seq, tm=GMLP_TM)
        else:
            xf = _attention(xf.reshape(bsz, seq, d), mod, w_q_b, w_o_b, k_pad, vt_pad, bias, ln_g, ln_b,
                            layer=l, j=l - N_A).reshape(bsz * seq, d)
        xf = ffn(xf, sub=2, half=1)
        if l == N_A - 1:
            k_pad, vt_pad = _kv_project(xf.reshape(bsz, seq, d), mkv, w_k_b, w_v_t_b)
    return xf.reshape(bsz, seq, d)
```

```python
import functools

import jax
import jax.numpy as jnp
from jax import lax
from jax.experimental import pallas as pl
from jax.experimental.pallas import tpu as pltpu

DEPTH = 4
CHUNK = 64
N_A = DEPTH // 2
N_B = DEPTH - N_A
GMLP_WINDOW = 128
GMLP_GROUPS = 8
N_HEADS = 16
LEFT_CHUNKS = 8
LEFT_PAD = LEFT_CHUNKS * CHUNK
MAX_REL = 4 * CHUNK
N_REL = (CHUNK - 1) + MAX_REL + 1
ALPHA = (2.0 * DEPTH) ** 0.25
LN_EPS = 1e-5
N_MOD = 9

V7X_LANES = 128
V7X_MXU_N = 256
V7X_BF16_SUBLANES = 16
V7X_VMEM_LIMIT_BYTES = 56 * 1024 * 1024

NEG_BIG = -1e30
LOG2E = 1.4426950408889634

KV_BLOCK = 256
ATTN_SUB_Q = 128
ATTN_HEADS_AHEAD = 3

FFN_TM = 512
FFN_FINISH_BLOCKS = 8
FFN_FIRST_FINISH_CHUNK = 1
GMLP_TM = 256
GMLP_FINISH_BLOCKS = 4
GMLP_FIRST_FINISH_GROUP = 4
ADALN_TN = 1024

F32 = jnp.float32
BF16 = jnp.bfloat16


def _params(n_axes):
    return pltpu.CompilerParams(
        dimension_semantics=("arbitrary",) * n_axes,
        vmem_limit_bytes=V7X_VMEM_LIMIT_BYTES,
    )


def _resident(block_shape, index_map):
    return pl.BlockSpec(block_shape, index_map, pipeline_mode=pl.Buffered(1))


def _layer_norm(r, g, b):
    mu = jnp.mean(r, axis=-1, keepdims=True)
    d = r - mu
    var = jnp.mean(d * d, axis=-1, keepdims=True)
    return d * lax.rsqrt(var + LN_EPS) * g + b


def _gelu(z):
    return 0.5 * z * (1.0 + lax.erf(z * (0.5 ** 0.5)))


def _modulate(x, mod_ref, sub):
    shift = mod_ref[3 * sub:3 * sub + 1, :]
    scale = mod_ref[3 * sub + 1:3 * sub + 2, :]
    return x * (1.0 + scale) + shift


def _gate(mod_ref, sub):
    return 1.0 + mod_ref[3 * sub + 2:3 * sub + 3, :]


def _zero_bits_of(v):
    bits = lax.bitcast_convert_type(v, jnp.int32)
    sixteen = jnp.full(bits.shape, 16, jnp.int32)
    cleared = lax.shift_right_logical(lax.shift_right_logical(bits, sixteen), sixteen)
    return jnp.max(cleared, axis=0, keepdims=True)


def _after(v, finished_rows):
    zero = _zero_bits_of(finished_rows)[:, :v.shape[1]]
    return lax.bitcast_convert_type(lax.bitcast_convert_type(v, jnp.int32) | zero, v.dtype)


def _finish_rows(sub, branch_scale, n_blocks, mod_prev_ref, lng_ref, lnb_ref, y_ref, xs_ref, o_ref, k):
    n = o_ref.shape[0] // n_blocks
    rows = slice(k * n, (k + 1) * n)
    r = ALPHA * xs_ref[rows, :] + (branch_scale * _gate(mod_prev_ref, sub)) * y_ref[rows, :]
    out = _layer_norm(r, lng_ref[sub:sub + 1, :], lnb_ref[sub:sub + 1, :])
    o_ref[rows, :] = out
    return out


def _pipelined(start, finish_rows, n_blocks, y_ref, xs_ref):
    i = pl.program_id(0)
    last = pl.num_programs(0) - 1

    @pl.when(i == 0)
    def _():
        y_ref[...] = jnp.zeros_like(y_ref)
        xs_ref[...] = jnp.zeros_like(xs_ref)

    @pl.when(i < last)
    def _():
        start(finish_rows)

    @pl.when(i == last)
    def _():
        for k in range(n_blocks):
            finish_rows(k)


def _pipelined_specs(n_tiles, tiles_per_seq, tm, d, layer):
    cur = lambda i: jnp.minimum(i, n_tiles - 1)
    prev = lambda i: jnp.maximum(i - 1, 0)
    in_specs = [
        pl.BlockSpec((tm, d), lambda i: (cur(i), 0)),
        pl.BlockSpec((None, None, N_MOD, d), lambda i: (layer, cur(i) // tiles_per_seq, 0, 0)),
        pl.BlockSpec((None, None, N_MOD, d), lambda i: (layer, prev(i) // tiles_per_seq, 0, 0)),
    ]
    out_spec = pl.BlockSpec((tm, d), lambda i: (prev(i), 0))
    parking = [pltpu.VMEM((tm, d), F32), pltpu.VMEM((tm, d), F32)]
    return in_specs, out_spec, parking


def _adaln_body(c_ref, w_ref, b_ref, o_ref):
    c = c_ref[...]
    c_act = c * jax.nn.sigmoid(c)
    o_ref[...] = jnp.dot(c_act, w_ref[...], preferred_element_type=F32) + b_ref[...]


def _adaln(c, w, b, tn):
    n_l, d, n = w.shape
    bsz = c.shape[0]
    return pl.pallas_call(
        _adaln_body,
        grid=(n_l, n // tn),
        in_specs=[
            pl.BlockSpec((bsz, d), lambda l, j: (0, 0)),
            pl.BlockSpec((None, d, tn), lambda l, j: (l, 0, j)),
            pl.BlockSpec((None, 1, tn), lambda l, j: (l, 0, j)),
        ],
        out_specs=pl.BlockSpec((None, bsz, tn), lambda l, j: (l, 0, j)),
        out_shape=jax.ShapeDtypeStruct((n_l, bsz, n), F32),
        compiler_params=_params(2),
        name="adaln",
    )(c, w, b.reshape(n_l, 1, n))


def _ffn_start(sub, d_ff, x_ref, mod_ref, wgu_ref, wd_ref, a_ref, y_ref, xs_ref, finish_rows):
    x = x_ref[...]
    h = _modulate(x, mod_ref, sub).astype(BF16)
    for j in range(d_ff // V7X_MXU_N):
        lo = j * V7X_MXU_N
        g = jnp.dot(h, wgu_ref[:, lo:lo + V7X_MXU_N], preferred_element_type=F32)
        u = jnp.dot(h, wgu_ref[:, d_ff + lo:d_ff + lo + V7X_MXU_N], preferred_element_type=F32)
        act = g * jax.nn.sigmoid(g) * u
        k = j - FFN_FIRST_FINISH_CHUNK
        if 0 <= k < FFN_FINISH_BLOCKS:
            act = _after(act, finish_rows(k))
        a_ref[:, lo:lo + V7X_MXU_N] = act.astype(BF16)
    y_ref[...] = jnp.dot(a_ref[...], wd_ref[...], preferred_element_type=F32)
    xs_ref[...] = x


def _ffn_body(sub, d_ff, x_ref, mod_ref, mod_prev_ref, wgu_ref, wd_ref, lng_ref, lnb_ref, o_ref,
              a_ref, y_ref, xs_ref):
    start = functools.partial(_ffn_start, sub, d_ff, x_ref, mod_ref, wgu_ref, wd_ref, a_ref, y_ref, xs_ref)
    finish_rows = functools.partial(_finish_rows, sub, 0.5, FFN_FINISH_BLOCKS, mod_prev_ref, lng_ref, lnb_ref,
                                    y_ref, xs_ref, o_ref)
    _pipelined(start, finish_rows, FFN_FINISH_BLOCKS, y_ref, xs_ref)


def _ffn(x, mod, w_gu, w_down, ln_g, ln_b, *, layer, sub, half, seq, tm):
    m, d = x.shape
    d_ff = w_down.shape[2]
    n_tiles = m // tm
    in_specs, out_spec, parking = _pipelined_specs(n_tiles, seq // tm, tm, d, layer)
    return pl.pallas_call(
        functools.partial(_ffn_body, sub, d_ff),
        grid=(n_tiles + 1,),
        in_specs=in_specs + [
            _resident((None, None, d, 2 * d_ff), lambda i: (layer, half, 0, 0)),
            _resident((None, None, d_ff, d), lambda i: (layer, half, 0, 0)),
            _resident((None, 3, d), lambda i: (layer, 0, 0)),
            _resident((None, 3, d), lambda i: (layer, 0, 0)),
        ],
        out_specs=out_spec,
        out_shape=jax.ShapeDtypeStruct((m, d), F32),
        scratch_shapes=[pltpu.VMEM((tm, d_ff), BF16)] + parking,
        compiler_params=_params(1),
        name="ffn",
    )(x, mod, mod, w_gu, w_down, ln_g, ln_b)


def _gmlp_start(x_ref, mod_ref, win_ref, bin_ref, glng_ref, glnb_ref, ws_ref, bst_ref, wout_ref,
                v_ref, m_ref, y_ref, xs_ref, finish_rows):
    tm = x_ref.shape[0]
    half = wout_ref.shape[0]
    gdim = half // GMLP_GROUPS
    x = x_ref[...]
    h = _modulate(x, mod_ref, 1).astype(BF16)

    for g in range(GMLP_GROUPS):
        lo = half + g * gdim
        z = jnp.dot(h, win_ref[:, lo:lo + gdim], preferred_element_type=F32) + bin_ref[:, lo:lo + gdim]
        v_ref[:, g * gdim:(g + 1) * gdim] = _gelu(z)
    v = v_ref[...]
    mu = jnp.mean(v, axis=-1, keepdims=True)
    dv = v - mu
    rstd = lax.rsqrt(jnp.mean(dv * dv, axis=-1, keepdims=True) + LN_EPS)

    t_out = lax.broadcasted_iota(jnp.int32, (GMLP_WINDOW, GMLP_WINDOW), 0)
    s_in = lax.broadcasted_iota(jnp.int32, (GMLP_WINDOW, GMLP_WINDOW), 1)
    causal = (s_in // CHUNK) <= (t_out // CHUNK)

    for g in range(GMLP_GROUPS):
        lo = g * gdim
        vn = ((v_ref[:, lo:lo + gdim] - mu) * rstd * glng_ref[:, lo:lo + gdim]
              + glnb_ref[:, lo:lo + gdim]).astype(BF16)
        ws_g = jnp.where(causal, ws_ref[g], 0.0).astype(BF16)
        b_col = bst_ref[:, g:g + 1]
        u = _gelu(jnp.dot(h, win_ref[:, lo:lo + gdim], preferred_element_type=F32)
                  + bin_ref[:, lo:lo + gdim])
        k = g - GMLP_FIRST_FINISH_GROUP
        if 0 <= k < GMLP_FINISH_BLOCKS:
            u = _after(u, finish_rows(k))
        for w in range(tm // GMLP_WINDOW):
            rows = slice(w * GMLP_WINDOW, (w + 1) * GMLP_WINDOW)
            s = jnp.dot(ws_g, vn[rows], preferred_element_type=F32) + b_col
            m_ref[rows, lo:lo + gdim] = (u[rows] * s).astype(BF16)

    y_ref[...] = jnp.dot(m_ref[...], wout_ref[...], preferred_element_type=F32)
    xs_ref[...] = x


def _gmlp_body(x_ref, mod_ref, mod_prev_ref, win_ref, bin_ref, glng_ref, glnb_ref, ws_ref, bst_ref, wout_ref,
               lng_ref, lnb_ref, o_ref, v_ref, m_ref, y_ref, xs_ref):
    start = functools.partial(_gmlp_start, x_ref, mod_ref, win_ref, bin_ref, glng_ref, glnb_ref, ws_ref, bst_ref,
                              wout_ref, v_ref, m_ref, y_ref, xs_ref)
    finish_rows = functools.partial(_finish_rows, 1, 1.0, GMLP_FINISH_BLOCKS, mod_prev_ref, lng_ref, lnb_ref,
                                    y_ref, xs_ref, o_ref)
    _pipelined(start, finish_rows, GMLP_FINISH_BLOCKS, y_ref, xs_ref)


def _gmlp(x, mod, w_in, b_in, gln_g, gln_b, w_s, b_s_t, w_out, ln_g, ln_b, *, layer, seq, tm):
    m, d = x.shape
    width = w_in.shape[2]
    half = width // 2
    n_tiles = m // tm
    in_specs, out_spec, parking = _pipelined_specs(n_tiles, seq // tm, tm, d, layer)
    const = lambda i: (layer, 0, 0)
    return pl.pallas_call(
        _gmlp_body,
        grid=(n_tiles + 1,),
        in_specs=in_specs + [
            _resident((None, d, width), const),
            _resident((None, 1, width), const),
            _resident((None, 1, half), const),
            _resident((None, 1, half), const),
            _resident((None, GMLP_GROUPS, GMLP_WINDOW, GMLP_WINDOW), lambda i: (layer, 0, 0, 0)),
            _resident((None, GMLP_WINDOW, GMLP_GROUPS), const),
            _resident((None, half, d), const),
            _resident((None, 3, d), const),
            _resident((None, 3, d), const),
        ],
        out_specs=out_spec,
        out_shape=jax.ShapeDtypeStruct((m, d), F32),
        scratch_shapes=[pltpu.VMEM((tm, half), F32), pltpu.VMEM((tm, half), BF16)] + parking,
        compiler_params=_params(1),
        name="gmlp",
    )(x, mod, mod, w_in, b_in, gln_g, gln_b, w_s, b_s_t, w_out, ln_g, ln_b)


def _kv_body(x_ref, mkv_ref, wk_ref, wvt_ref, k_ref, vt_ref):
    i = pl.program_id(1)

    @pl.when(i == 0)
    def _():
        k_ref[...] = jnp.zeros_like(k_ref)
        vt_ref[...] = jnp.zeros_like(vt_ref)

    @pl.when(i > 0)
    def _():
        h = (x_ref[...] * (1.0 + mkv_ref[1:2, :]) + mkv_ref[0:1, :]).astype(BF16)
        k = jnp.dot(h, wk_ref[...], preferred_element_type=F32)
        k_ref[...] = k.astype(BF16).reshape(k_ref.shape)
        vt = lax.dot_general(wvt_ref[...], h, (((1,), (1,)), ((), ())), preferred_element_type=F32)
        for n in range(vt_ref.shape[0]):
            vt_ref[n] = vt[:, n * KV_BLOCK:(n + 1) * KV_BLOCK].astype(BF16)


def _kv_project(x, mkv, w_k, w_v_t):
    bsz, seq, d = x.shape
    tm = LEFT_PAD
    blocks_per_tile = tm // KV_BLOCK
    n_blocks = (LEFT_PAD + seq) // KV_BLOCK
    return pl.pallas_call(
        _kv_body,
        grid=(bsz, 1 + seq // tm),
        in_specs=[
            pl.BlockSpec((None, tm, d), lambda b, i: (b, jnp.maximum(i - 1, 0), 0)),
            pl.BlockSpec((None, 2, d), lambda b, i: (b, 0, 0)),
            _resident((d, d), lambda b, i: (0, 0)),
            _resident((d, d), lambda b, i: (0, 0)),
        ],
        out_specs=[
            pl.BlockSpec((None, blocks_per_tile, KV_BLOCK, d), lambda b, i: (b, i, 0, 0)),
            pl.BlockSpec((None, blocks_per_tile, d, KV_BLOCK), lambda b, i: (b, i, 0, 0)),
        ],
        out_shape=[
            jax.ShapeDtypeStruct((bsz, n_blocks, KV_BLOCK, d), BF16),
            jax.ShapeDtypeStruct((bsz, n_blocks, d, KV_BLOCK), BF16),
        ],
        compiler_params=_params(2),
        name="kv_project",
    )(x, mkv, w_k, w_v_t)


def _bias_body(rb_ref, o_ref):
    tq = o_ref.shape[2]
    rb = rb_ref[...]
    hi = rb.astype(BF16)
    rem = rb - hi.astype(F32)
    mid = rem.astype(BF16)
    lo = (rem - mid.astype(F32)).astype(BF16)
    t = lax.broadcasted_iota(jnp.int32, (1, tq), 1)
    rel = lax.broadcasted_iota(jnp.int32, (N_REL, tq), 0)
    rows = o_ref.shape[1]
    for rr in range(rows):
        r = pl.program_id(0) * rows + rr
        idx = jnp.clip(t + LEFT_PAD - r, -(CHUNK - 1), MAX_REL) + (CHUNK - 1)
        onehot = (rel == idx).astype(BF16)
        val = (jnp.dot(hi, onehot, preferred_element_type=F32)
               + jnp.dot(mid, onehot, preferred_element_type=F32)
               + jnp.dot(lo, onehot, preferred_element_type=F32))
        qc = t // CHUNK
        kc = r // CHUNK
        visible = (kc >= qc) & (kc <= qc + LEFT_CHUNKS)
        o_ref[:, rr, :] = jnp.where(visible, val * LOG2E, NEG_BIG)


def _bias_table(rel_bias, tq):
    nh = rel_bias.shape[0]
    tk = tq + LEFT_PAD
    rows = 8
    return pl.pallas_call(
        _bias_body,
        grid=(tk // rows,),
        in_specs=[pl.BlockSpec((nh, N_REL), lambda i: (0, 0))],
        out_specs=pl.BlockSpec((nh, rows, tq), lambda i: (0, i, 0)),
        out_shape=jax.ShapeDtypeStruct((nh, tk, tq), F32),
        compiler_params=_params(1),
        name="bias_table",
    )(rel_bias)


def _attn_step(mask_padding, x_ref, mod_ref, wq_ref, wo_ref, k_ref, vt_ref, bias_ref, lng_ref, lnb_ref,
               o_ref, ctxt_ref):
    tq, d = x_ref.shape
    win_blocks = (tq + LEFT_PAD) // KV_BLOCK
    tk = win_blocks * KV_BLOCK
    sub_k = ATTN_SUB_Q + LEFT_PAD
    hd = d // N_HEADS
    i = pl.program_id(1)
    x = x_ref[...]
    h = _modulate(x, mod_ref, 1)
    q = jnp.dot(h.astype(BF16), wq_ref[...], preferred_element_type=F32) * (hd ** -0.5 * LOG2E)
    q = q.astype(BF16)

    first_key_row = LEFT_PAD - i * tq
    first_head = lax.broadcasted_iota(jnp.int32, (tq, V7X_LANES), 1) < hd
    ones_rows = jnp.ones((V7X_BF16_SUBLANES, KV_BLOCK), BF16)

    def scores_t(head):
        lanes = slice(head // 2 * V7X_LANES, (head // 2 + 1) * V7X_LANES)
        q_pair = q[:, lanes]
        k_win = k_ref[pl.ds(i, win_blocks), :, lanes].reshape(tk, V7X_LANES)
        q_one = jnp.where(first_head if head % 2 == 0 else ~first_head, q_pair, jnp.zeros_like(q_pair))
        return lax.dot_general(k_win, q_one, (((1,), (1,)), ((), ())), preferred_element_type=F32)

    def softmax_t(head, s_t):
        probs = []
        for r0 in range(0, tq, ATTN_SUB_Q):
            s_sub = s_t[r0:r0 + sub_k, r0:r0 + ATTN_SUB_Q] + bias_ref[head]
            if mask_padding:
                row = lax.broadcasted_iota(jnp.int32, (sub_k, 1), 0)
                s_sub = jnp.where(row >= first_key_row - r0, s_sub, NEG_BIG)
            s_max = jnp.max(s_sub, axis=0, keepdims=True)
            pieces = [jnp.exp2(s_sub - s_max).astype(BF16)]
            if r0:
                pieces.insert(0, jnp.zeros((r0, ATTN_SUB_Q), BF16))
            if tk - sub_k - r0:
                pieces.append(jnp.zeros((tk - sub_k - r0, ATTN_SUB_Q), BF16))
            probs.append(jnp.concatenate(pieces, axis=0))
        return jnp.concatenate(probs, axis=1)

    def context_t(head, p_t):
        rows = slice(head * hd, (head + 1) * hd)
        acc = None
        for n in range(win_blocks):
            v_t = vt_ref[pl.ds(i + n, 1), rows, :].reshape(hd, KV_BLOCK)
            lhs = jnp.concatenate([v_t, ones_rows], axis=0)
            part = jnp.dot(lhs, p_t[n * KV_BLOCK:(n + 1) * KV_BLOCK], preferred_element_type=F32)
            acc = part if acc is None else acc + part
        ctxt_ref[rows, :] = (acc[:hd] / acc[hd:hd + 1]).astype(BF16)

    pending = [scores_t(head) for head in range(ATTN_HEADS_AHEAD)]
    for head in range(N_HEADS):
        s_cur = pending.pop(0)
        if head + ATTN_HEADS_AHEAD < N_HEADS:
            pending.append(scores_t(head + ATTN_HEADS_AHEAD))
        context_t(head, softmax_t(head, s_cur))

    y = lax.dot_general(ctxt_ref[...], wo_ref[...], (((0,), (0,)), ((), ())), preferred_element_type=F32)
    r = ALPHA * x + _gate(mod_ref, 1) * y
    o_ref[...] = _layer_norm(r, lng_ref[1:2, :], lnb_ref[1:2, :])


def _attn_body(x_ref, *refs):
    has_padding = pl.program_id(1) * x_ref.shape[0] < LEFT_PAD
    pl.when(has_padding)(functools.partial(_attn_step, True, x_ref, *refs))
    pl.when(jnp.logical_not(has_padding))(functools.partial(_attn_step, False, x_ref, *refs))


def _attention(x, mod, w_q, w_o, k_pad, vt_pad, bias, ln_g, ln_b, *, layer, j):
    bsz, seq, d = x.shape
    tq = KV_BLOCK
    kv_blocks = k_pad.shape[1]
    return pl.pallas_call(
        _attn_body,
        grid=(bsz, seq // tq),
        in_specs=[
            pl.BlockSpec((None, tq, d), lambda b, i: (b, i, 0)),
            pl.BlockSpec((None, None, N_MOD, d), lambda b, i: (layer, b, 0, 0)),
            _resident((None, d, d), lambda b, i: (j, 0, 0)),
            _resident((None, d, d), lambda b, i: (j, 0, 0)),
            pl.BlockSpec((None, kv_blocks, KV_BLOCK, d), lambda b, i: (b, 0, 0, 0)),
            pl.BlockSpec((None, kv_blocks, d, KV_BLOCK), lambda b, i: (b, 0, 0, 0)),
            _resident((N_HEADS, ATTN_SUB_Q + LEFT_PAD, ATTN_SUB_Q), lambda b, i: (j, 0, 0)),
            _resident((None, 3, d), lambda b, i: (layer, 0, 0)),
            _resident((None, 3, d), lambda b, i: (layer, 0, 0)),
        ],
        out_specs=pl.BlockSpec((None, tq, d), lambda b, i: (b, i, 0)),
        out_shape=jax.ShapeDtypeStruct((bsz, seq, d), F32),
        scratch_shapes=[pltpu.VMEM((d, tq), BF16)],
        compiler_params=_params(2),
        name="attention",
    )(x, mod, w_q, w_o, k_pad, vt_pad, bias, ln_g, ln_b)


def kernel(x, c, w_ada, b_ada, ln_g, ln_b, ffn_gu, ffn_down, gmlp_w_in, gmlp_b_in, gmlp_ln_g, gmlp_ln_b,
           gmlp_w_s, gmlp_b_s, gmlp_w_out, w_ada_kv, b_ada_kv, w_kv, attn_w_q, attn_rel_bias, attn_w_o):
    bsz, seq, d = x.shape
    mod = _adaln(c, w_ada, b_ada, ADALN_TN).reshape(DEPTH, bsz, N_MOD, d)
    mkv = _adaln(c, w_ada_kv[None], b_ada_kv[None], ADALN_TN).reshape(bsz, 2, d)

    ffn_gu_b = ffn_gu.astype(BF16)
    ffn_down_b = ffn_down.astype(BF16)
    w_in_b = gmlp_w_in.astype(BF16)
    w_out_b = gmlp_w_out.astype(BF16)
    w_k_b = w_kv[:, :d].astype(BF16)
    w_v_t_b = w_kv[:, d:].T.astype(BF16)
    w_q_b = attn_w_q.astype(BF16)
    w_o_b = attn_w_o.astype(BF16)
    b_in = gmlp_b_in[:, None, :]
    gln_g = gmlp_ln_g[:, None, :]
    gln_b = gmlp_ln_b[:, None, :]
    b_s_t = jnp.swapaxes(gmlp_b_s, 1, 2)
    bias = _bias_table(attn_rel_bias.reshape(N_B * N_HEADS, N_REL), ATTN_SUB_Q)

    xf = x.reshape(bsz * seq, d)
    k_pad = vt_pad = None
    for l in range(DEPTH):
        ffn = functools.partial(_ffn, mod=mod, w_gu=ffn_gu_b, w_down=ffn_down_b, ln_g=ln_g, ln_b=ln_b,
                                layer=l, seq=seq, tm=FFN_TM)
        xf = ffn(xf, sub=0, half=0)
        if l < N_A:
            xf = _gmlp(xf, mod, w_in_b, b_in, gln_g, gln_b, gmlp_w_s, b_s_t, w_out_b, ln_g, ln_b,
                       layer=l, seq=seq, tm=GMLP_TM)
        else:
            xf = _attention(xf.reshape(bsz, seq, d), mod, w_q_b, w_o_b, k_pad, vt_pad, bias, ln_g, ln_b,
                            layer=l, j=l - N_A).reshape(bsz * seq, d)
        xf = ffn(xf, sub=2, half=1)
        if l == N_A - 1:
            k_pad, vt_pad = _kv_project(xf.reshape(bsz, seq, d), mkv, w_k_b, w_v_t_b)
    return xf.reshape(bsz, seq, d)
```

```python
import functools

import jax
import jax.numpy as jnp
from jax import lax
from jax.experimental import pallas as pl
from jax.experimental.pallas import tpu as pltpu

DEPTH = 4
CHUNK = 64
N_A = DEPTH // 2
N_B = DEPTH - N_A
GMLP_WINDOW = 128
GMLP_GROUPS = 8
N_HEADS = 16
LEFT_CHUNKS = 8
LEFT_PAD = LEFT_CHUNKS * CHUNK
MAX_REL = 4 * CHUNK
N_REL = (CHUNK - 1) + MAX_REL + 1
ALPHA = (2.0 * DEPTH) ** 0.25
LN_EPS = 1e-5
N_MOD = 9

V7X_LANES = 128
V7X_MXU_N = 256
V7X_BF16_SUBLANES = 16
V7X_VMEM_LIMIT_BYTES = 56 * 1024 * 1024

NEG_BIG = -1e30
LOG2E = 1.4426950408889634

KV_BLOCK = 256
ATTN_SUB_Q = 128
ATTN_HEADS_AHEAD = 3

FFN_TM = 512
FFN_FINISH_BLOCKS = 8
FFN_FIRST_FINISH_CHUNK = 1
GMLP_TM = 256
ADALN_TN = 1024

F32 = jnp.float32
BF16 = jnp.bfloat16


def _params(n_axes):
    return pltpu.CompilerParams(
        dimension_semantics=("arbitrary",) * n_axes,
        vmem_limit_bytes=V7X_VMEM_LIMIT_BYTES,
    )


def _resident(block_shape, index_map):
    return pl.BlockSpec(block_shape, index_map, pipeline_mode=pl.Buffered(1))


def _layer_norm(r, g, b):
    mu = jnp.mean(r, axis=-1, keepdims=True)
    d = r - mu
    var = jnp.mean(d * d, axis=-1, keepdims=True)
    return d * lax.rsqrt(var + LN_EPS) * g + b


def _gelu(z):
    return 0.5 * z * (1.0 + lax.erf(z * (0.5 ** 0.5)))


def _modulate(x, mod_ref, sub):
    shift = mod_ref[3 * sub:3 * sub + 1, :]
    scale = mod_ref[3 * sub + 1:3 * sub + 2, :]
    return x * (1.0 + scale) + shift


def _gate(mod_ref, sub):
    return 1.0 + mod_ref[3 * sub + 2:3 * sub + 3, :]


def _zero_bits_of(v):
    bits = lax.bitcast_convert_type(v, jnp.int32)
    sixteen = jnp.full(bits.shape, 16, jnp.int32)
    cleared = lax.shift_right_logical(lax.shift_right_logical(bits, sixteen), sixteen)
    return jnp.max(cleared, axis=0, keepdims=True)


def _after(v, finished_rows):
    zero = _zero_bits_of(finished_rows)[:, :v.shape[1]]
    return jnp.where(zero == 0, v, jnp.zeros_like(v))


def _finish_rows(sub, branch_scale, n_blocks, mod_prev_ref, lng_ref, lnb_ref, y_ref, xs_ref, o_ref, k):
    n = o_ref.shape[0] // n_blocks
    rows = slice(k * n, (k + 1) * n)
    r = ALPHA * xs_ref[rows, :] + (branch_scale * _gate(mod_prev_ref, sub)) * y_ref[rows, :]
    out = _layer_norm(r, lng_ref[sub:sub + 1, :], lnb_ref[sub:sub + 1, :])
    o_ref[rows, :] = out
    return out


def _pipelined(start, finish_rows, n_blocks, y_ref, xs_ref):
    i = pl.program_id(0)
    last = pl.num_programs(0) - 1

    @pl.when(i == 0)
    def _():
        y_ref[...] = jnp.zeros_like(y_ref)
        xs_ref[...] = jnp.zeros_like(xs_ref)

    @pl.when(i < last)
    def _():
        start(finish_rows)

    @pl.when(i == last)
    def _():
        for k in range(n_blocks):
            finish_rows(k)


def _pipelined_specs(n_tiles, tiles_per_seq, tm, d, layer):
    cur = lambda i: jnp.minimum(i, n_tiles - 1)
    prev = lambda i: jnp.maximum(i - 1, 0)
    in_specs = [
        pl.BlockSpec((tm, d), lambda i: (cur(i), 0)),
        pl.BlockSpec((None, None, N_MOD, d), lambda i: (layer, cur(i) // tiles_per_seq, 0, 0)),
        pl.BlockSpec((None, None, N_MOD, d), lambda i: (layer, prev(i) // tiles_per_seq, 0, 0)),
    ]
    out_spec = pl.BlockSpec((tm, d), lambda i: (prev(i), 0))
    parking = [pltpu.VMEM((tm, d), F32), pltpu.VMEM((tm, d), F32)]
    return in_specs, out_spec, parking


def _adaln_body(c_ref, w_ref, b_ref, o_ref):
    c = c_ref[...]
    c_act = c * jax.nn.sigmoid(c)
    o_ref[...] = jnp.dot(c_act, w_ref[...], preferred_element_type=F32) + b_ref[...]


def _adaln(c, w, b, tn):
    n_l, d, n = w.shape
    bsz = c.shape[0]
    return pl.pallas_call(
        _adaln_body,
        grid=(n_l, n // tn),
        in_specs=[
            pl.BlockSpec((bsz, d), lambda l, j: (0, 0)),
            pl.BlockSpec((None, d, tn), lambda l, j: (l, 0, j)),
            pl.BlockSpec((None, 1, tn), lambda l, j: (l, 0, j)),
        ],
        out_specs=pl.BlockSpec((None, bsz, tn), lambda l, j: (l, 0, j)),
        out_shape=jax.ShapeDtypeStruct((n_l, bsz, n), F32),
        compiler_params=_params(2),
        name="adaln",
    )(c, w, b.reshape(n_l, 1, n))


def _ffn_start(sub, d_ff, x_ref, mod_ref, wgu_ref, wd_ref, a_ref, y_ref, xs_ref, finish_rows):
    x = x_ref[...]
    h = _modulate(x, mod_ref, sub).astype(BF16)
    for j in range(d_ff // V7X_MXU_N):
        lo = j * V7X_MXU_N
        g = jnp.dot(h, wgu_ref[:, lo:lo + V7X_MXU_N], preferred_element_type=F32)
        u = jnp.dot(h, wgu_ref[:, d_ff + lo:d_ff + lo + V7X_MXU_N], preferred_element_type=F32)
        act = g * jax.nn.sigmoid(g) * u
        k = j - FFN_FIRST_FINISH_CHUNK
        if 0 <= k < FFN_FINISH_BLOCKS:
            act = _after(act, finish_rows(k))
        a_ref[:, lo:lo + V7X_MXU_N] = act.astype(BF16)
    y_ref[...] = jnp.dot(a_ref[...], wd_ref[...], preferred_element_type=F32)
    xs_ref[...] = x


def _ffn_body(sub, d_ff, x_ref, mod_ref, mod_prev_ref, wgu_ref, wd_ref, lng_ref, lnb_ref, o_ref,
              a_ref, y_ref, xs_ref):
    start = functools.partial(_ffn_start, sub, d_ff, x_ref, mod_ref, wgu_ref, wd_ref, a_ref, y_ref, xs_ref)
    finish_rows = functools.partial(_finish_rows, sub, 0.5, FFN_FINISH_BLOCKS, mod_prev_ref, lng_ref, lnb_ref,
                                    y_ref, xs_ref, o_ref)
    _pipelined(start, finish_rows, FFN_FINISH_BLOCKS, y_ref, xs_ref)


def _ffn(x, mod, w_gu, w_down, ln_g, ln_b, *, layer, sub, half, seq, tm):
    m, d = x.shape
    d_ff = w_down.shape[2]
    n_tiles = m // tm
    in_specs, out_spec, parking = _pipelined_specs(n_tiles, seq // tm, tm, d, layer)
    return pl.pallas_call(
        functools.partial(_ffn_body, sub, d_ff),
        grid=(n_tiles + 1,),
        in_specs=in_specs + [
            _resident((None, None, d, 2 * d_ff), lambda i: (layer, half, 0, 0)),
            _resident((None, None, d_ff, d), lambda i: (layer, half, 0, 0)),
            _resident((None, 3, d), lambda i: (layer, 0, 0)),
            _resident((None, 3, d), lambda i: (layer, 0, 0)),
        ],
        out_specs=out_spec,
        out_shape=jax.ShapeDtypeStruct((m, d), F32),
        scratch_shapes=[pltpu.VMEM((tm, d_ff), BF16)] + parking,
        compiler_params=_params(1),
        name="ffn",
    )(x, mod, mod, w_gu, w_down, ln_g, ln_b)


def _gmlp_body(x_ref, mod_ref, win_ref, bin_ref, glng_ref, glnb_ref, ws_ref, bst_ref, wout_ref,
               lng_ref, lnb_ref, o_ref, v_ref, m_ref):
    tm = x_ref.shape[0]
    half = wout_ref.shape[0]
    gdim = half // GMLP_GROUPS
    x = x_ref[...]
    h = _modulate(x, mod_ref, 1).astype(BF16)

    for g in range(GMLP_GROUPS):
        lo = half + g * gdim
        z = jnp.dot(h, win_ref[:, lo:lo + gdim], preferred_element_type=F32) + bin_ref[:, lo:lo + gdim]
        v_ref[:, g * gdim:(g + 1) * gdim] = _gelu(z)
    v = v_ref[...]
    mu = jnp.mean(v, axis=-1, keepdims=True)
    dv = v - mu
    rstd = lax.rsqrt(jnp.mean(dv * dv, axis=-1, keepdims=True) + LN_EPS)

    t_out = lax.broadcasted_iota(jnp.int32, (GMLP_WINDOW, GMLP_WINDOW), 0)
    s_in = lax.broadcasted_iota(jnp.int32, (GMLP_WINDOW, GMLP_WINDOW), 1)
    causal = (s_in // CHUNK) <= (t_out // CHUNK)

    for g in range(GMLP_GROUPS):
        lo = g * gdim
        vn = ((v_ref[:, lo:lo + gdim] - mu) * rstd * glng_ref[:, lo:lo + gdim]
              + glnb_ref[:, lo:lo + gdim]).astype(BF16)
        ws_g = jnp.where(causal, ws_ref[g], 0.0).astype(BF16)
        b_col = bst_ref[:, g:g + 1]
        u = _gelu(jnp.dot(h, win_ref[:, lo:lo + gdim], preferred_element_type=F32)
                  + bin_ref[:, lo:lo + gdim])
        for w in range(tm // GMLP_WINDOW):
            rows = slice(w * GMLP_WINDOW, (w + 1) * GMLP_WINDOW)
            s = jnp.dot(ws_g, vn[rows], preferred_element_type=F32) + b_col
            m_ref[rows, lo:lo + gdim] = (u[rows] * s).astype(BF16)

    y = jnp.dot(m_ref[...], wout_ref[...], preferred_element_type=F32)
    r = ALPHA * x + _gate(mod_ref, 1) * y
    o_ref[...] = _layer_norm(r, lng_ref[1:2, :], lnb_ref[1:2, :])


def _gmlp(x, mod, w_in, b_in, gln_g, gln_b, w_s, b_s_t, w_out, ln_g, ln_b, *, layer, seq, tm):
    m, d = x.shape
    width = w_in.shape[2]
    half = width // 2
    tiles_per_seq = seq // tm
    const = lambda i: (layer, 0, 0)
    return pl.pallas_call(
        _gmlp_body,
        grid=(m // tm,),
        in_specs=[
            pl.BlockSpec((tm, d), lambda i: (i, 0)),
            pl.BlockSpec((None, None, N_MOD, d), lambda i: (layer, i // tiles_per_seq, 0, 0)),
            _resident((None, d, width), const),
            _resident((None, 1, width), const),
            _resident((None, 1, half), const),
            _resident((None, 1, half), const),
            _resident((None, GMLP_GROUPS, GMLP_WINDOW, GMLP_WINDOW), lambda i: (layer, 0, 0, 0)),
            _resident((None, GMLP_WINDOW, GMLP_GROUPS), const),
            _resident((None, half, d), const),
            _resident((None, 3, d), const),
            _resident((None, 3, d), const),
        ],
        out_specs=pl.BlockSpec((tm, d), lambda i: (i, 0)),
        out_shape=jax.ShapeDtypeStruct((m, d), F32),
        scratch_shapes=[pltpu.VMEM((tm, half), F32), pltpu.VMEM((tm, half), BF16)],
        compiler_params=_params(1),
        name="gmlp",
    )(x, mod, w_in, b_in, gln_g, gln_b, w_s, b_s_t, w_out, ln_g, ln_b)


def _kv_body(x_ref, mkv_ref, wk_ref, wvt_ref, k_ref, vt_ref):
    i = pl.program_id(1)

    @pl.when(i == 0)
    def _():
        k_ref[...] = jnp.zeros_like(k_ref)
        vt_ref[...] = jnp.zeros_like(vt_ref)

    @pl.when(i > 0)
    def _():
        h = (x_ref[...] * (1.0 + mkv_ref[1:2, :]) + mkv_ref[0:1, :]).astype(BF16)
        k = jnp.dot(h, wk_ref[...], preferred_element_type=F32)
        k_ref[...] = k.astype(BF16).reshape(k_ref.shape)
        vt = lax.dot_general(wvt_ref[...], h, (((1,), (1,)), ((), ())), preferred_element_type=F32)
        for n in range(vt_ref.shape[0]):
            vt_ref[n] = vt[:, n * KV_BLOCK:(n + 1) * KV_BLOCK].astype(BF16)


def _kv_project(x, mkv, w_k, w_v_t):
    bsz, seq, d = x.shape
    tm = LEFT_PAD
    blocks_per_tile = tm // KV_BLOCK
    n_blocks = (LEFT_PAD + seq) // KV_BLOCK
    return pl.pallas_call(
        _kv_body,
        grid=(bsz, 1 + seq // tm),
        in_specs=[
            pl.BlockSpec((None, tm, d), lambda b, i: (b, jnp.maximum(i - 1, 0), 0)),
            pl.BlockSpec((None, 2, d), lambda b, i: (b, 0, 0)),
            _resident((d, d), lambda b, i: (0, 0)),
            _resident((d, d), lambda b, i: (0, 0)),
        ],
        out_specs=[
            pl.BlockSpec((None, blocks_per_tile, KV_BLOCK, d), lambda b, i: (b, i, 0, 0)),
            pl.BlockSpec((None, blocks_per_tile, d, KV_BLOCK), lambda b, i: (b, i, 0, 0)),
        ],
        out_shape=[
            jax.ShapeDtypeStruct((bsz, n_blocks, KV_BLOCK, d), BF16),
            jax.ShapeDtypeStruct((bsz, n_blocks, d, KV_BLOCK), BF16),
        ],
        compiler_params=_params(2),
        name="kv_project",
    )(x, mkv, w_k, w_v_t)


def _bias_body(rb_ref, o_ref):
    tq = o_ref.shape[2]
    rb = rb_ref[...]
    hi = rb.astype(BF16)
    rem = rb - hi.astype(F32)
    mid = rem.astype(BF16)
    lo = (rem - mid.astype(F32)).astype(BF16)
    t = lax.broadcasted_iota(jnp.int32, (1, tq), 1)
    rel = lax.broadcasted_iota(jnp.int32, (N_REL, tq), 0)
    rows = o_ref.shape[1]
    for rr in range(rows):
        r = pl.program_id(0) * rows + rr
        idx = jnp.clip(t + LEFT_PAD - r, -(CHUNK - 1), MAX_REL) + (CHUNK - 1)
        onehot = (rel == idx).astype(BF16)
        val = (jnp.dot(hi, onehot, preferred_element_type=F32)
               + jnp.dot(mid, onehot, preferred_element_type=F32)
               + jnp.dot(lo, onehot, preferred_element_type=F32))
        qc = t // CHUNK
        kc = r // CHUNK
        visible = (kc >= qc) & (kc <= qc + LEFT_CHUNKS)
        o_ref[:, rr, :] = jnp.where(visible, val * LOG2E, NEG_BIG)


def _bias_table(rel_bias, tq):
    nh = rel_bias.shape[0]
    tk = tq + LEFT_PAD
    rows = 8
    return pl.pallas_call(
        _bias_body,
        grid=(tk // rows,),
        in_specs=[pl.BlockSpec((nh, N_REL), lambda i: (0, 0))],
        out_specs=pl.BlockSpec((nh, rows, tq), lambda i: (0, i, 0)),
        out_shape=jax.ShapeDtypeStruct((nh, tk, tq), F32),
        compiler_params=_params(1),
        name="bias_table",
    )(rel_bias)


def _attn_step(mask_padding, x_ref, mod_ref, wq_ref, wo_ref, k_ref, vt_ref, bias_ref, lng_ref, lnb_ref,
               o_ref, ctxt_ref):
    tq, d = x_ref.shape
    win_blocks = (tq + LEFT_PAD) // KV_BLOCK
    tk = win_blocks * KV_BLOCK
    sub_k = ATTN_SUB_Q + LEFT_PAD
    hd = d // N_HEADS
    i = pl.program_id(1)
    x = x_ref[...]
    h = _modulate(x, mod_ref, 1)
    q = jnp.dot(h.astype(BF16), wq_ref[...], preferred_element_type=F32) * (hd ** -0.5 * LOG2E)
    q = q.astype(BF16)

    first_key_row = LEFT_PAD - i * tq
    first_head = lax.broadcasted_iota(jnp.int32, (tq, V7X_LANES), 1) < hd
    ones_rows = jnp.ones((V7X_BF16_SUBLANES, KV_BLOCK), BF16)

    def scores_t(head):
        lanes = slice(head // 2 * V7X_LANES, (head // 2 + 1) * V7X_LANES)
        q_pair = q[:, lanes]
        k_win = k_ref[pl.ds(i, win_blocks), :, lanes].reshape(tk, V7X_LANES)
        q_one = jnp.where(first_head if head % 2 == 0 else ~first_head, q_pair, jnp.zeros_like(q_pair))
        return lax.dot_general(k_win, q_one, (((1,), (1,)), ((), ())), preferred_element_type=F32)

    def softmax_t(head, s_t):
        probs = []
        for r0 in range(0, tq, ATTN_SUB_Q):
            s_sub = s_t[r0:r0 + sub_k, r0:r0 + ATTN_SUB_Q] + bias_ref[head]
            if mask_padding:
                row = lax.broadcasted_iota(jnp.int32, (sub_k, 1), 0)
                s_sub = jnp.where(row >= first_key_row - r0, s_sub, NEG_BIG)
            s_max = jnp.max(s_sub, axis=0, keepdims=True)
            pieces = [jnp.exp2(s_sub - s_max).astype(BF16)]
            if r0:
                pieces.insert(0, jnp.zeros((r0, ATTN_SUB_Q), BF16))
            if tk - sub_k - r0:
                pieces.append(jnp.zeros((tk - sub_k - r0, ATTN_SUB_Q), BF16))
            probs.append(jnp.concatenate(pieces, axis=0))
        return jnp.concatenate(probs, axis=1)

    def context_t(head, p_t):
        rows = slice(head * hd, (head + 1) * hd)
        acc = None
        for n in range(win_blocks):
            v_t = vt_ref[pl.ds(i + n, 1), rows, :].reshape(hd, KV_BLOCK)
            lhs = jnp.concatenate([v_t, ones_rows], axis=0)
            part = jnp.dot(lhs, p_t[n * KV_BLOCK:(n + 1) * KV_BLOCK], preferred_element_type=F32)
            acc = part if acc is None else acc + part
        ctxt_ref[rows, :] = (acc[:hd] / acc[hd:hd + 1]).astype(BF16)

    pending = [scores_t(head) for head in range(ATTN_HEADS_AHEAD)]
    for head in range(N_HEADS):
        s_cur = pending.pop(0)
        if head + ATTN_HEADS_AHEAD < N_HEADS:
            pending.append(scores_t(head + ATTN_HEADS_AHEAD))
        context_t(head, softmax_t(head, s_cur))

    y = lax.dot_general(ctxt_ref[...], wo_ref[...], (((0,), (0,)), ((), ())), preferred_element_type=F32)
    r = ALPHA * x + _gate(mod_ref, 1) * y
    o_ref[...] = _layer_norm(r, lng_ref[1:2, :], lnb_ref[1:2, :])


def _attn_body(x_ref, *refs):
    has_padding = pl.program_id(1) * x_ref.shape[0] < LEFT_PAD
    pl.when(has_padding)(functools.partial(_attn_step, True, x_ref, *refs))
    pl.when(jnp.logical_not(has_padding))(functools.partial(_attn_step, False, x_ref, *refs))


def _attention(x, mod, w_q, w_o, k_pad, vt_pad, bias, ln_g, ln_b, *, layer, j):
    bsz, seq, d = x.shape
    tq = KV_BLOCK
    kv_blocks = k_pad.shape[1]
    return pl.pallas_call(
        _attn_body,
        grid=(bsz, seq // tq),
        in_specs=[
            pl.BlockSpec((None, tq, d), lambda b, i: (b, i, 0)),
            pl.BlockSpec((None, None, N_MOD, d), lambda b, i: (layer, b, 0, 0)),
            _resident((None, d, d), lambda b, i: (j, 0, 0)),
            _resident((None, d, d), lambda b, i: (j, 0, 0)),
            pl.BlockSpec((None, kv_blocks, KV_BLOCK, d), lambda b, i: (b, 0, 0, 0)),
            pl.BlockSpec((None, kv_blocks, d, KV_BLOCK), lambda b, i: (b, 0, 0, 0)),
            _resident((N_HEADS, ATTN_SUB_Q + LEFT_PAD, ATTN_SUB_Q), lambda b, i: (j, 0, 0)),
            _resident((None, 3, d), lambda b, i: (layer, 0, 0)),
            _resident((None, 3, d), lambda b, i: (layer, 0, 0)),
        ],
        out_specs=pl.BlockSpec((None, tq, d), lambda b, i: (b, i, 0)),
        out_shape=jax.ShapeDtypeStruct((bsz, seq, d), F32),
        scratch_shapes=[pltpu.VMEM((d, tq), BF16)],
        compiler_params=_params(2),
        name="attention",
    )(x, mod, w_q, w_o, k_pad, vt_pad, bias, ln_g, ln_b)


def kernel(x, c, w_ada, b_ada, ln_g, ln_b, ffn_gu, ffn_down, gmlp_w_in, gmlp_b_in, gmlp_ln_g, gmlp_ln_b,
           gmlp_w_s, gmlp_b_s, gmlp_w_out, w_ada_kv, b_ada_kv, w_kv, attn_w_q, attn_rel_bias, attn_w_o):
    bsz, seq, d = x.shape
    mod = _adaln(c, w_ada, b_ada, ADALN_TN).reshape(DEPTH, bsz, N_MOD, d)
    mkv = _adaln(c, w_ada_kv[None], b_ada_kv[None], ADALN_TN).reshape(bsz, 2, d)

    ffn_gu_b = ffn_gu.astype(BF16)
    ffn_down_b = ffn_down.astype(BF16)
    w_in_b = gmlp_w_in.astype(BF16)
    w_out_b = gmlp_w_out.astype(BF16)
    w_k_b = w_kv[:, :d].astype(BF16)
    w_v_t_b = w_kv[:, d:].T.astype(BF16)
    w_q_b = attn_w_q.astype(BF16)
    w_o_b = attn_w_o.astype(BF16)
    b_in = gmlp_b_in[:, None, :]
    gln_g = gmlp_ln_g[:, None, :]
    gln_b = gmlp_ln_b[:, None, :]
    b_s_t = jnp.swapaxes(gmlp_b_s, 1, 2)
    bias = _bias_table(attn_rel_bias.reshape(N_B * N_HEADS, N_REL), ATTN_SUB_Q)

    xf = x.reshape(bsz * seq, d)
    k_pad = vt_pad = None
    for l in range(DEPTH):
        ffn = functools.partial(_ffn, mod=mod, w_gu=ffn_gu_b, w_down=ffn_down_b, ln_g=ln_g, ln_b=ln_b,
                                layer=l, seq=seq, tm=FFN_TM)
        xf = ffn(xf, sub=0, half=0)
        if l < N_A:
            xf = _gmlp(xf, mod, w_in_b, b_in, gln_g, gln_b, gmlp_w_s, b_s_t, w_out_b, ln_g, ln_b,
                       layer=l, seq=seq, tm=GMLP_TM)
        else:
            xf = _attention(xf.reshape(bsz, seq, d), mod, w_q_b, w_o_b, k_pad, vt_pad, bias, ln_g, ln_b,
                            layer=l, j=l - N_A).reshape(bsz * seq, d)
        xf = ffn(xf, sub=2, half=1)
        if l == N_A - 1:
            k_pad, vt_pad = _kv_project(xf.reshape(bsz, seq, d), mkv, w_k_b, w_v_t_b)
    return xf.reshape(bsz, seq, d)
```

```python
import functools

import jax
import jax.numpy as jnp
from jax import lax
from jax.experimental import pallas as pl
from jax.experimental.pallas import tpu as pltpu

DEPTH = 4
CHUNK = 64
N_A = DEPTH // 2
N_B = DEPTH - N_A
GMLP_WINDOW = 128
GMLP_GROUPS = 8
N_HEADS = 16
LEFT_CHUNKS = 8
LEFT_PAD = LEFT_CHUNKS * CHUNK
MAX_REL = 4 * CHUNK
N_REL = (CHUNK - 1) + MAX_REL + 1
ALPHA = (2.0 * DEPTH) ** 0.25
LN_EPS = 1e-5
N_MOD = 9

V7X_LANES = 128
V7X_MXU_N = 256
V7X_BF16_SUBLANES = 16
V7X_VMEM_LIMIT_BYTES = 56 * 1024 * 1024

NEG_BIG = -1e30
LOG2E = 1.4426950408889634

KV_BLOCK = 256
ATTN_SUB_Q = 128
ATTN_HEADS_AHEAD = 3

FFN_TM = 512
FFN_FINISH_BLOCKS = 8
FFN_FIRST_FINISH_CHUNK = 1
FFN_WEIGHT_SLOTS = 3
GMLP_TM = 256
GMLP_DOTS_AHEAD = 3
ADALN_TN = 1024

F32 = jnp.float32
BF16 = jnp.bfloat16


def _params(n_axes):
    return pltpu.CompilerParams(
        dimension_semantics=("arbitrary",) * n_axes,
        vmem_limit_bytes=V7X_VMEM_LIMIT_BYTES,
    )


def _resident(block_shape, index_map):
    return pl.BlockSpec(block_shape, index_map, pipeline_mode=pl.Buffered(1))


def _layer_norm(r, g, b):
    mu = jnp.mean(r, axis=-1, keepdims=True)
    d = r - mu
    var = jnp.mean(d * d, axis=-1, keepdims=True)
    return d * lax.rsqrt(var + LN_EPS) * g + b


def _gelu(z):
    return 0.5 * z * (1.0 + lax.erf(z * (0.5 ** 0.5)))


def _modulate(x, mod_ref, sub):
    shift = mod_ref[3 * sub:3 * sub + 1, :]
    scale = mod_ref[3 * sub + 1:3 * sub + 2, :]
    return x * (1.0 + scale) + shift


def _gate(mod_ref, sub):
    return 1.0 + mod_ref[3 * sub + 2:3 * sub + 3, :]


def _zero_bits_of(v):
    bits = lax.bitcast_convert_type(v, jnp.int32)
    sixteen = jnp.full(bits.shape, 16, jnp.int32)
    cleared = lax.shift_right_logical(lax.shift_right_logical(bits, sixteen), sixteen)
    return jnp.max(cleared, axis=0, keepdims=True)


def _after(v, finished_rows):
    zero = _zero_bits_of(finished_rows)[:, :v.shape[1]]
    return jnp.where(zero == 0, v, jnp.zeros_like(v))


def _finish_rows(sub, branch_scale, n_blocks, mod_prev_ref, lng_ref, lnb_ref, y_ref, xs_ref, o_ref, k):
    n = o_ref.shape[0] // n_blocks
    rows = slice(k * n, (k + 1) * n)
    r = ALPHA * xs_ref[rows, :] + (branch_scale * _gate(mod_prev_ref, sub)) * y_ref[rows, :]
    out = _layer_norm(r, lng_ref[sub:sub + 1, :], lnb_ref[sub:sub + 1, :])
    o_ref[rows, :] = out
    return out


def _pipelined_specs(n_tiles, tiles_per_seq, tm, d, layer):
    cur = lambda i: jnp.minimum(i, n_tiles - 1)
    prev = lambda i: jnp.maximum(i - 1, 0)
    in_specs = [
        pl.BlockSpec((tm, d), lambda i: (cur(i), 0)),
        pl.BlockSpec((None, None, N_MOD, d), lambda i: (layer, cur(i) // tiles_per_seq, 0, 0)),
        pl.BlockSpec((None, None, N_MOD, d), lambda i: (layer, prev(i) // tiles_per_seq, 0, 0)),
    ]
    out_spec = pl.BlockSpec((tm, d), lambda i: (prev(i), 0))
    parking = [pltpu.VMEM((tm, d), F32), pltpu.VMEM((tm, d), F32)]
    return in_specs, out_spec, parking


def _adaln_body(c_ref, w_ref, b_ref, o_ref):
    c = c_ref[...]
    c_act = c * jax.nn.sigmoid(c)
    o_ref[...] = jnp.dot(c_act, w_ref[...], preferred_element_type=F32) + b_ref[...]


def _adaln(c, w, b, tn):
    n_l, d, n = w.shape
    bsz = c.shape[0]
    return pl.pallas_call(
        _adaln_body,
        grid=(n_l, n // tn),
        in_specs=[
            pl.BlockSpec((bsz, d), lambda l, j: (0, 0)),
            pl.BlockSpec((None, d, tn), lambda l, j: (l, 0, j)),
            pl.BlockSpec((None, 1, tn), lambda l, j: (l, 0, j)),
        ],
        out_specs=pl.BlockSpec((None, bsz, tn), lambda l, j: (l, 0, j)),
        out_shape=jax.ShapeDtypeStruct((n_l, bsz, n), F32),
        compiler_params=_params(2),
        name="adaln",
    )(c, w, b.reshape(n_l, 1, n))


def _ffn_weight_copies(layer, half, d_ff, wgu_hbm, wd_hbm, stage_gu_ref, stage_d_ref, sem_ref, j):
    lo = j * V7X_MXU_N
    slot = j % FFN_WEIGHT_SLOTS
    return (
        pltpu.make_async_copy(wgu_hbm.at[layer, half, :, pl.ds(lo, V7X_MXU_N)],
                              stage_gu_ref.at[slot, 0], sem_ref.at[slot, 0]),
        pltpu.make_async_copy(wgu_hbm.at[layer, half, :, pl.ds(d_ff + lo, V7X_MXU_N)],
                              stage_gu_ref.at[slot, 1], sem_ref.at[slot, 1]),
        pltpu.make_async_copy(wd_hbm.at[layer, half, pl.ds(lo, V7X_MXU_N), :],
                              stage_d_ref.at[slot], sem_ref.at[slot, 2]),
    )


def _ffn_take_chunk(copies, n_chunks, d_ff, wgu_ref, wd_ref, stage_gu_ref, stage_d_ref, j):
    for copy in copies(j):
        copy.wait()
    lo = j * V7X_MXU_N
    slot = j % FFN_WEIGHT_SLOTS
    wgu_ref[:, lo:lo + V7X_MXU_N] = stage_gu_ref[slot, 0].astype(BF16)
    wgu_ref[:, d_ff + lo:d_ff + lo + V7X_MXU_N] = stage_gu_ref[slot, 1].astype(BF16)
    wd_ref[lo:lo + V7X_MXU_N, :] = stage_d_ref[slot].astype(BF16)
    if j + FFN_WEIGHT_SLOTS < n_chunks:
        for copy in copies(j + FFN_WEIGHT_SLOTS):
            copy.start()


def _ffn_start(sub, d_ff, x_ref, mod_ref, wgu_ref, wd_ref, a_ref, y_ref, xs_ref, finish_rows=None, take_chunk=None):
    x = x_ref[...]
    h = _modulate(x, mod_ref, sub).astype(BF16)
    for j in range(d_ff // V7X_MXU_N):
        lo = j * V7X_MXU_N
        if take_chunk is not None:
            take_chunk(j)
        g = jnp.dot(h, wgu_ref[:, lo:lo + V7X_MXU_N], preferred_element_type=F32)
        u = jnp.dot(h, wgu_ref[:, d_ff + lo:d_ff + lo + V7X_MXU_N], preferred_element_type=F32)
        act = g * jax.nn.sigmoid(g) * u
        k = j - FFN_FIRST_FINISH_CHUNK
        if finish_rows is not None and 0 <= k < FFN_FINISH_BLOCKS:
            act = _after(act, finish_rows(k))
        a_ref[:, lo:lo + V7X_MXU_N] = act.astype(BF16)
    y_ref[...] = jnp.dot(a_ref[...], wd_ref[...], preferred_element_type=F32)
    xs_ref[...] = x


def _ffn_body(layer, half, sub, d_ff, x_ref, mod_ref, mod_prev_ref, wgu_hbm, wd_hbm, lng_ref, lnb_ref, o_ref,
              wgu_ref, wd_ref, stage_gu_ref, stage_d_ref, sem_ref, a_ref, y_ref, xs_ref):
    i = pl.program_id(0)
    last = pl.num_programs(0) - 1
    n_chunks = d_ff // V7X_MXU_N
    start = functools.partial(_ffn_start, sub, d_ff, x_ref, mod_ref, wgu_ref, wd_ref, a_ref, y_ref, xs_ref)
    finish_rows = functools.partial(_finish_rows, sub, 0.5, FFN_FINISH_BLOCKS, mod_prev_ref, lng_ref, lnb_ref,
                                    y_ref, xs_ref, o_ref)
    copies = functools.partial(_ffn_weight_copies, layer, half, d_ff, wgu_hbm, wd_hbm, stage_gu_ref, stage_d_ref,
                               sem_ref)
    take_chunk = functools.partial(_ffn_take_chunk, copies, n_chunks, d_ff, wgu_ref, wd_ref, stage_gu_ref,
                                   stage_d_ref)

    @pl.when(i == 0)
    def _():
        for j in range(FFN_WEIGHT_SLOTS):
            for copy in copies(j):
                copy.start()
        start(take_chunk=take_chunk)

    @pl.when((i > 0) & (i < last))
    def _():
        start(finish_rows=finish_rows)

    @pl.when(i == last)
    def _():
        for k in range(FFN_FINISH_BLOCKS):
            finish_rows(k)


def _ffn(x, mod, w_gu, w_down, ln_g, ln_b, *, layer, sub, half, seq, tm):
    m, d = x.shape
    d_ff = w_down.shape[2]
    n_tiles = m // tm
    in_specs, out_spec, parking = _pipelined_specs(n_tiles, seq // tm, tm, d, layer)
    return pl.pallas_call(
        functools.partial(_ffn_body, layer, half, sub, d_ff),
        grid=(n_tiles + 1,),
        in_specs=in_specs + [
            pl.BlockSpec(memory_space=pl.ANY),
            pl.BlockSpec(memory_space=pl.ANY),
            _resident((None, 3, d), lambda i: (layer, 0, 0)),
            _resident((None, 3, d), lambda i: (layer, 0, 0)),
        ],
        out_specs=out_spec,
        out_shape=jax.ShapeDtypeStruct((m, d), F32),
        scratch_shapes=[
            pltpu.VMEM((d, 2 * d_ff), BF16),
            pltpu.VMEM((d_ff, d), BF16),
            pltpu.VMEM((FFN_WEIGHT_SLOTS, 2, d, V7X_MXU_N), F32),
            pltpu.VMEM((FFN_WEIGHT_SLOTS, V7X_MXU_N, d), F32),
            pltpu.SemaphoreType.DMA((FFN_WEIGHT_SLOTS, 3)),
            pltpu.VMEM((tm, d_ff), BF16),
        ] + parking,
        compiler_params=_params(1),
        name="ffn",
    )(x, mod, mod, w_gu, w_down, ln_g, ln_b)


def _gmlp_body(x_ref, mod_ref, win_ref, bin_ref, glng_ref, glnb_ref, ws_ref, bst_ref, wout_ref,
               lng_ref, lnb_ref, o_ref, v_ref, m_ref):
    tm = x_ref.shape[0]
    half = wout_ref.shape[0]
    gdim = half // GMLP_GROUPS
    x = x_ref[...]
    h = _modulate(x, mod_ref, 1).astype(BF16)

    def project(lo):
        return jnp.dot(h, win_ref[:, lo:lo + gdim], preferred_element_type=F32) + bin_ref[:, lo:lo + gdim]

    order = [half + g * gdim for g in range(GMLP_GROUPS)] + [g * gdim for g in range(GMLP_GROUPS)]
    pending = [project(lo) for lo in order[:GMLP_DOTS_AHEAD]]

    def next_projection(n):
        z = pending.pop(0)
        if n + GMLP_DOTS_AHEAD < len(order):
            pending.append(project(order[n + GMLP_DOTS_AHEAD]))
        return z

    for g in range(GMLP_GROUPS):
        v_ref[:, g * gdim:(g + 1) * gdim] = _gelu(next_projection(g))
    v = v_ref[...]
    mu = jnp.mean(v, axis=-1, keepdims=True)
    dv = v - mu
    rstd = lax.rsqrt(jnp.mean(dv * dv, axis=-1, keepdims=True) + LN_EPS)

    t_out = lax.broadcasted_iota(jnp.int32, (GMLP_WINDOW, GMLP_WINDOW), 0)
    s_in = lax.broadcasted_iota(jnp.int32, (GMLP_WINDOW, GMLP_WINDOW), 1)
    causal = (s_in // CHUNK) <= (t_out // CHUNK)

    for g in range(GMLP_GROUPS):
        lo = g * gdim
        vn = ((v_ref[:, lo:lo + gdim] - mu) * rstd * glng_ref[:, lo:lo + gdim]
              + glnb_ref[:, lo:lo + gdim]).astype(BF16)
        ws_g = jnp.where(causal, ws_ref[g], 0.0).astype(BF16)
        b_col = bst_ref[:, g:g + 1]
        u = _gelu(next_projection(GMLP_GROUPS + g))
        for w in range(tm // GMLP_WINDOW):
            rows = slice(w * GMLP_WINDOW, (w + 1) * GMLP_WINDOW)
            s = jnp.dot(ws_g, vn[rows], preferred_element_type=F32) + b_col
            m_ref[rows, lo:lo + gdim] = (u[rows] * s).astype(BF16)

    y = jnp.dot(m_ref[...], wout_ref[...], preferred_element_type=F32)
    r = ALPHA * x + _gate(mod_ref, 1) * y
    o_ref[...] = _layer_norm(r, lng_ref[1:2, :], lnb_ref[1:2, :])


def _gmlp(x, mod, w_in, b_in, gln_g, gln_b, w_s, b_s_t, w_out, ln_g, ln_b, *, layer, seq, tm):
    m, d = x.shape
    width = w_in.shape[2]
    half = width // 2
    tiles_per_seq = seq // tm
    const = lambda i: (layer, 0, 0)
    return pl.pallas_call(
        _gmlp_body,
        grid=(m // tm,),
        in_specs=[
            pl.BlockSpec((tm, d), lambda i: (i, 0)),
            pl.BlockSpec((None, None, N_MOD, d), lambda i: (layer, i // tiles_per_seq, 0, 0)),
            _resident((None, d, width), const),
            _resident((None, 1, width), const),
            _resident((None, 1, half), const),
            _resident((None, 1, half), const),
            _resident((None, GMLP_GROUPS, GMLP_WINDOW, GMLP_WINDOW), lambda i: (layer, 0, 0, 0)),
            _resident((None, GMLP_WINDOW, GMLP_GROUPS), const),
            _resident((None, half, d), const),
            _resident((None, 3, d), const),
            _resident((None, 3, d), const),
        ],
        out_specs=pl.BlockSpec((tm, d), lambda i: (i, 0)),
        out_shape=jax.ShapeDtypeStruct((m, d), F32),
        scratch_shapes=[pltpu.VMEM((tm, half), F32), pltpu.VMEM((tm, half), BF16)],
        compiler_params=_params(1),
        name="gmlp",
    )(x, mod, w_in, b_in, gln_g, gln_b, w_s, b_s_t, w_out, ln_g, ln_b)


def _kv_body(x_ref, mkv_ref, wk_ref, wvt_ref, k_ref, vt_ref):
    i = pl.program_id(1)

    @pl.when(i == 0)
    def _():
        k_ref[...] = jnp.zeros_like(k_ref)
        vt_ref[...] = jnp.zeros_like(vt_ref)

    @pl.when(i > 0)
    def _():
        h = (x_ref[...] * (1.0 + mkv_ref[1:2, :]) + mkv_ref[0:1, :]).astype(BF16)
        k = jnp.dot(h, wk_ref[...], preferred_element_type=F32)
        k_ref[...] = k.astype(BF16).reshape(k_ref.shape)
        vt = lax.dot_general(wvt_ref[...], h, (((1,), (1,)), ((), ())), preferred_element_type=F32)
        for n in range(vt_ref.shape[0]):
            vt_ref[n] = vt[:, n * KV_BLOCK:(n + 1) * KV_BLOCK].astype(BF16)


def _kv_project(x, mkv, w_k, w_v_t):
    bsz, seq, d = x.shape
    tm = LEFT_PAD
    blocks_per_tile = tm // KV_BLOCK
    n_blocks = (LEFT_PAD + seq) // KV_BLOCK
    return pl.pallas_call(
        _kv_body,
        grid=(bsz, 1 + seq // tm),
        in_specs=[
            pl.BlockSpec((None, tm, d), lambda b, i: (b, jnp.maximum(i - 1, 0), 0)),
            pl.BlockSpec((None, 2, d), lambda b, i: (b, 0, 0)),
            _resident((d, d), lambda b, i: (0, 0)),
            _resident((d, d), lambda b, i: (0, 0)),
        ],
        out_specs=[
            pl.BlockSpec((None, blocks_per_tile, KV_BLOCK, d), lambda b, i: (b, i, 0, 0)),
            pl.BlockSpec((None, blocks_per_tile, d, KV_BLOCK), lambda b, i: (b, i, 0, 0)),
        ],
        out_shape=[
            jax.ShapeDtypeStruct((bsz, n_blocks, KV_BLOCK, d), BF16),
            jax.ShapeDtypeStruct((bsz, n_blocks, d, KV_BLOCK), BF16),
        ],
        compiler_params=_params(2),
        name="kv_project",
    )(x, mkv, w_k, w_v_t)


def _bias_body(rb_ref, o_ref):
    tq = o_ref.shape[2]
    rb = rb_ref[...]
    hi = rb.astype(BF16)
    rem = rb - hi.astype(F32)
    mid = rem.astype(BF16)
    lo = (rem - mid.astype(F32)).astype(BF16)
    t = lax.broadcasted_iota(jnp.int32, (1, tq), 1)
    rel = lax.broadcasted_iota(jnp.int32, (N_REL, tq), 0)
    rows = o_ref.shape[1]
    for rr in range(rows):
        r = pl.program_id(0) * rows + rr
        idx = jnp.clip(t + LEFT_PAD - r, -(CHUNK - 1), MAX_REL) + (CHUNK - 1)
        onehot = (rel == idx).astype(BF16)
        val = (jnp.dot(hi, onehot, preferred_element_type=F32)
               + jnp.dot(mid, onehot, preferred_element_type=F32)
               + jnp.dot(lo, onehot, preferred_element_type=F32))
        qc = t // CHUNK
        kc = r // CHUNK
        visible = (kc >= qc) & (kc <= qc + LEFT_CHUNKS)
        o_ref[:, rr, :] = jnp.where(visible, val * LOG2E, NEG_BIG)


def _bias_table(rel_bias, tq):
    nh = rel_bias.shape[0]
    tk = tq + LEFT_PAD
    rows = 8
    return pl.pallas_call(
        _bias_body,
        grid=(tk // rows,),
        in_specs=[pl.BlockSpec((nh, N_REL), lambda i: (0, 0))],
        out_specs=pl.BlockSpec((nh, rows, tq), lambda i: (0, i, 0)),
        out_shape=jax.ShapeDtypeStruct((nh, tk, tq), F32),
        compiler_params=_params(1),
        name="bias_table",
    )(rel_bias)


def _attn_step(mask_padding, x_ref, mod_ref, wq_ref, wo_ref, k_ref, vt_ref, bias_ref, lng_ref, lnb_ref,
               o_ref, ctxt_ref):
    tq, d = x_ref.shape
    win_blocks = (tq + LEFT_PAD) // KV_BLOCK
    tk = win_blocks * KV_BLOCK
    sub_k = ATTN_SUB_Q + LEFT_PAD
    hd = d // N_HEADS
    i = pl.program_id(1)
    x = x_ref[...]
    h = _modulate(x, mod_ref, 1)
    q = jnp.dot(h.astype(BF16), wq_ref[...], preferred_element_type=F32) * (hd ** -0.5 * LOG2E)
    q = q.astype(BF16)

    first_key_row = LEFT_PAD - i * tq
    first_head = lax.broadcasted_iota(jnp.int32, (tq, V7X_LANES), 1) < hd
    ones_rows = jnp.ones((V7X_BF16_SUBLANES, KV_BLOCK), BF16)

    def scores_t(head):
        lanes = slice(head // 2 * V7X_LANES, (head // 2 + 1) * V7X_LANES)
        q_pair = q[:, lanes]
        k_win = k_ref[pl.ds(i, win_blocks), :, lanes].reshape(tk, V7X_LANES)
        q_one = jnp.where(first_head if head % 2 == 0 else ~first_head, q_pair, jnp.zeros_like(q_pair))
        return lax.dot_general(k_win, q_one, (((1,), (1,)), ((), ())), preferred_element_type=F32)

    def softmax_t(head, s_t):
        probs = []
        for r0 in range(0, tq, ATTN_SUB_Q):
            s_sub = s_t[r0:r0 + sub_k, r0:r0 + ATTN_SUB_Q] + bias_ref[head]
            if mask_padding:
                row = lax.broadcasted_iota(jnp.int32, (sub_k, 1), 0)
                s_sub = jnp.where(row >= first_key_row - r0, s_sub, NEG_BIG)
            s_max = jnp.max(s_sub, axis=0, keepdims=True)
            pieces = [jnp.exp2(s_sub - s_max).astype(BF16)]
            if r0:
                pieces.insert(0, jnp.zeros((r0, ATTN_SUB_Q), BF16))
            if tk - sub_k - r0:
                pieces.append(jnp.zeros((tk - sub_k - r0, ATTN_SUB_Q), BF16))
            probs.append(jnp.concatenate(pieces, axis=0))
        return jnp.concatenate(probs, axis=1)

    def context_t(head, p_t):
        rows = slice(head * hd, (head + 1) * hd)
        acc = None
        for n in range(win_blocks):
            v_t = vt_ref[pl.ds(i + n, 1), rows, :].reshape(hd, KV_BLOCK)
            lhs = jnp.concatenate([v_t, ones_rows], axis=0)
            part = jnp.dot(lhs, p_t[n * KV_BLOCK:(n + 1) * KV_BLOCK], preferred_element_type=F32)
            acc = part if acc is None else acc + part
        ctxt_ref[rows, :] = (acc[:hd] / acc[hd:hd + 1]).astype(BF16)

    pending = [scores_t(head) for head in range(ATTN_HEADS_AHEAD)]
    for head in range(N_HEADS):
        s_cur = pending.pop(0)
        if head + ATTN_HEADS_AHEAD < N_HEADS:
            pending.append(scores_t(head + ATTN_HEADS_AHEAD))
        context_t(head, softmax_t(head, s_cur))

    y = lax.dot_general(ctxt_ref[...], wo_ref[...], (((0,), (0,)), ((), ())), preferred_element_type=F32)
    r = ALPHA * x + _gate(mod_ref, 1) * y
    o_ref[...] = _layer_norm(r, lng_ref[1:2, :], lnb_ref[1:2, :])


def _attn_body(x_ref, *refs):
    has_padding = pl.program_id(1) * x_ref.shape[0] < LEFT_PAD
    pl.when(has_padding)(functools.partial(_attn_step, True, x_ref, *refs))
    pl.when(jnp.logical_not(has_padding))(functools.partial(_attn_step, False, x_ref, *refs))


def _attention(x, mod, w_q, w_o, k_pad, vt_pad, bias, ln_g, ln_b, *, layer, j):
    bsz, seq, d = x.shape
    tq = KV_BLOCK
    kv_blocks = k_pad.shape[1]
    return pl.pallas_call(
        _attn_body,
        grid=(bsz, seq // tq),
        in_specs=[
            pl.BlockSpec((None, tq, d), lambda b, i: (b, i, 0)),
            pl.BlockSpec((None, None, N_MOD, d), lambda b, i: (layer, b, 0, 0)),
            _resident((None, d, d), lambda b, i: (j, 0, 0)),
            _resident((None, d, d), lambda b, i: (j, 0, 0)),
            pl.BlockSpec((None, kv_blocks, KV_BLOCK, d), lambda b, i: (b, 0, 0, 0)),
            pl.BlockSpec((None, kv_blocks, d, KV_BLOCK), lambda b, i: (b, 0, 0, 0)),
            _resident((N_HEADS, ATTN_SUB_Q + LEFT_PAD, ATTN_SUB_Q), lambda b, i: (j, 0, 0)),
            _resident((None, 3, d), lambda b, i: (layer, 0, 0)),
            _resident((None, 3, d), lambda b, i: (layer, 0, 0)),
        ],
        out_specs=pl.BlockSpec((None, tq, d), lambda b, i: (b, i, 0)),
        out_shape=jax.ShapeDtypeStruct((bsz, seq, d), F32),
        scratch_shapes=[pltpu.VMEM((d, tq), BF16)],
        compiler_params=_params(2),
        name="attention",
    )(x, mod, w_q, w_o, k_pad, vt_pad, bias, ln_g, ln_b)


def kernel(x, c, w_ada, b_ada, ln_g, ln_b, ffn_gu, ffn_down, gmlp_w_in, gmlp_b_in, gmlp_ln_g, gmlp_ln_b,
           gmlp_w_s, gmlp_b_s, gmlp_w_out, w_ada_kv, b_ada_kv, w_kv, attn_w_q, attn_rel_bias, attn_w_o):
    bsz, seq, d = x.shape
    mod = _adaln(c, w_ada, b_ada, ADALN_TN).reshape(DEPTH, bsz, N_MOD, d)
    mkv = _adaln(c, w_ada_kv[None], b_ada_kv[None], ADALN_TN).reshape(bsz, 2, d)

    w_in_b = gmlp_w_in.astype(BF16)
    w_out_b = gmlp_w_out.astype(BF16)
    w_k_b = w_kv[:, :d].astype(BF16)
    w_v_t_b = w_kv[:, d:].T.astype(BF16)
    w_q_b = attn_w_q.astype(BF16)
    w_o_b = attn_w_o.astype(BF16)
    b_in = gmlp_b_in[:, None, :]
    gln_g = gmlp_ln_g[:, None, :]
    gln_b = gmlp_ln_b[:, None, :]
    b_s_t = jnp.swapaxes(gmlp_b_s, 1, 2)
    bias = _bias_table(attn_rel_bias.reshape(N_B * N_HEADS, N_REL), ATTN_SUB_Q)

    xf = x.reshape(bsz * seq, d)
    k_pad = vt_pad = None
    for l in range(DEPTH):
        ffn = functools.partial(_ffn, mod=mod, w_gu=ffn_gu, w_down=ffn_down, ln_g=ln_g, ln_b=ln_b,
                                layer=l, seq=seq, tm=FFN_TM)
        xf = ffn(xf, sub=0, half=0)
        if l < N_A:
            xf = _gmlp(xf, mod, w_in_b, b_in, gln_g, gln_b, gmlp_w_s, b_s_t, w_out_b, ln_g, ln_b,
                       layer=l, seq=seq, tm=GMLP_TM)
        else:
            xf = _attention(xf.reshape(bsz, seq, d), mod, w_q_b, w_o_b, k_pad, vt_pad, bias, ln_g, ln_b,
                            layer=l, j=l - N_A).reshape(bsz * seq, d)
        xf = ffn(xf, sub=2, half=1)
        if l == N_A - 1:
            k_pad, vt_pad = _kv_project(xf.reshape(bsz, seq, d), mkv, w_k_b, w_v_t_b)
    return xf.reshape(bsz, seq, d)
```

```python
import functools

import jax
import jax.numpy as jnp
from jax import lax
from jax.experimental import pallas as pl
from jax.experimental.pallas import tpu as pltpu

DEPTH = 4
CHUNK = 64
N_A = DEPTH // 2
N_B = DEPTH - N_A
GMLP_WINDOW = 128
GMLP_GROUPS = 8
N_HEADS = 16
LEFT_CHUNKS = 8
LEFT_PAD = LEFT_CHUNKS * CHUNK
MAX_REL = 4 * CHUNK
N_REL = (CHUNK - 1) + MAX_REL + 1
ALPHA = (2.0 * DEPTH) ** 0.25
LN_EPS = 1e-5
N_MOD = 9

V7X_LANES = 128
V7X_MXU_N = 256
V7X_BF16_SUBLANES = 16
V7X_VMEM_LIMIT_BYTES = 56 * 1024 * 1024

NEG_BIG = -1e30
LOG2E = 1.4426950408889634

KV_BLOCK = 256
ATTN_SUB_Q = 128
ATTN_HEADS_AHEAD = 3

FFN_TM = 512
FFN_FINISH_BLOCKS = 8
FFN_FIRST_FINISH_CHUNK = 1
FFN_WEIGHT_SLOTS = 3
GMLP_TM = 512
GMLP_DOTS_AHEAD = 3
ADALN_TN = 1024
BIAS_ROWS_PER_STEP = 64

F32 = jnp.float32
BF16 = jnp.bfloat16


def _params(n_axes):
    return pltpu.CompilerParams(
        dimension_semantics=("arbitrary",) * n_axes,
        vmem_limit_bytes=V7X_VMEM_LIMIT_BYTES,
    )


def _resident(block_shape, index_map):
    return pl.BlockSpec(block_shape, index_map, pipeline_mode=pl.Buffered(1))


def _layer_norm(r, g, b):
    mu = jnp.mean(r, axis=-1, keepdims=True)
    d = r - mu
    var = jnp.mean(d * d, axis=-1, keepdims=True)
    return d * lax.rsqrt(var + LN_EPS) * g + b


def _gelu(z):
    return 0.5 * z * (1.0 + lax.erf(z * (0.5 ** 0.5)))


def _modulate(x, mod_ref, sub):
    shift = mod_ref[3 * sub:3 * sub + 1, :]
    scale = mod_ref[3 * sub + 1:3 * sub + 2, :]
    return x * (1.0 + scale) + shift


def _gate(mod_ref, sub):
    return 1.0 + mod_ref[3 * sub + 2:3 * sub + 3, :]


def _zero_bits_of(v):
    bits = lax.bitcast_convert_type(v, jnp.int32)
    sixteen = jnp.full(bits.shape, 16, jnp.int32)
    cleared = lax.shift_right_logical(lax.shift_right_logical(bits, sixteen), sixteen)
    return jnp.max(cleared, axis=0, keepdims=True)


def _after(v, finished_rows):
    zero = _zero_bits_of(finished_rows)[:, :v.shape[1]]
    return jnp.where(zero == 0, v, jnp.zeros_like(v))


def _finish_rows(sub, branch_scale, n_blocks, mod_prev_ref, lng_ref, lnb_ref, y_ref, xs_ref, o_ref, k):
    n = o_ref.shape[0] // n_blocks
    rows = slice(k * n, (k + 1) * n)
    r = ALPHA * xs_ref[rows, :] + (branch_scale * _gate(mod_prev_ref, sub)) * y_ref[rows, :]
    out = _layer_norm(r, lng_ref[sub:sub + 1, :], lnb_ref[sub:sub + 1, :])
    o_ref[rows, :] = out
    return out


def _pipelined_specs(n_tiles, tiles_per_seq, tm, d, layer):
    cur = lambda i: jnp.minimum(i, n_tiles - 1)
    prev = lambda i: jnp.maximum(i - 1, 0)
    in_specs = [
        pl.BlockSpec((tm, d), lambda i: (cur(i), 0)),
        pl.BlockSpec((None, None, N_MOD, d), lambda i: (layer, cur(i) // tiles_per_seq, 0, 0)),
        pl.BlockSpec((None, None, N_MOD, d), lambda i: (layer, prev(i) // tiles_per_seq, 0, 0)),
    ]
    out_spec = pl.BlockSpec((tm, d), lambda i: (prev(i), 0))
    parking = [pltpu.VMEM((tm, d), F32), pltpu.VMEM((tm, d), F32)]
    return in_specs, out_spec, parking


def _adaln_body(c_ref, w_ref, b_ref, o_ref):
    c = c_ref[...]
    c_act = c * jax.nn.sigmoid(c)
    o_ref[...] = jnp.dot(c_act, w_ref[...], preferred_element_type=F32) + b_ref[...]


def _adaln(c, w, b, tn):
    n_l, d, n = w.shape
    bsz = c.shape[0]
    return pl.pallas_call(
        _adaln_body,
        grid=(n_l, n // tn),
        in_specs=[
            pl.BlockSpec((bsz, d), lambda l, j: (0, 0)),
            pl.BlockSpec((None, d, tn), lambda l, j: (l, 0, j)),
            pl.BlockSpec((None, 1, tn), lambda l, j: (l, 0, j)),
        ],
        out_specs=pl.BlockSpec((None, bsz, tn), lambda l, j: (l, 0, j)),
        out_shape=jax.ShapeDtypeStruct((n_l, bsz, n), F32),
        compiler_params=_params(2),
        name="adaln",
    )(c, w, b.reshape(n_l, 1, n))


def _ffn_weight_copies(layer, half, d_ff, wgu_hbm, wd_hbm, stage_gu_ref, stage_d_ref, sem_ref, j):
    lo = j * V7X_MXU_N
    slot = j % FFN_WEIGHT_SLOTS
    return (
        pltpu.make_async_copy(wgu_hbm.at[layer, half, :, pl.ds(lo, V7X_MXU_N)],
                              stage_gu_ref.at[slot, 0], sem_ref.at[slot, 0]),
        pltpu.make_async_copy(wgu_hbm.at[layer, half, :, pl.ds(d_ff + lo, V7X_MXU_N)],
                              stage_gu_ref.at[slot, 1], sem_ref.at[slot, 1]),
        pltpu.make_async_copy(wd_hbm.at[layer, half, pl.ds(lo, V7X_MXU_N), :],
                              stage_d_ref.at[slot], sem_ref.at[slot, 2]),
    )


def _ffn_take_chunk(copies, n_chunks, d_ff, wgu_ref, wd_ref, stage_gu_ref, stage_d_ref, j):
    for copy in copies(j):
        copy.wait()
    lo = j * V7X_MXU_N
    slot = j % FFN_WEIGHT_SLOTS
    wgu_ref[:, lo:lo + V7X_MXU_N] = stage_gu_ref[slot, 0].astype(BF16)
    wgu_ref[:, d_ff + lo:d_ff + lo + V7X_MXU_N] = stage_gu_ref[slot, 1].astype(BF16)
    wd_ref[lo:lo + V7X_MXU_N, :] = stage_d_ref[slot].astype(BF16)
    if j + FFN_WEIGHT_SLOTS < n_chunks:
        for copy in copies(j + FFN_WEIGHT_SLOTS):
            copy.start()


def _ffn_start(sub, d_ff, x_ref, mod_ref, wgu_ref, wd_ref, a_ref, y_ref, xs_ref, finish_rows=None, take_chunk=None):
    x = x_ref[...]
    h = _modulate(x, mod_ref, sub).astype(BF16)
    for j in range(d_ff // V7X_MXU_N):
        lo = j * V7X_MXU_N
        if take_chunk is not None:
            take_chunk(j)
        g = jnp.dot(h, wgu_ref[:, lo:lo + V7X_MXU_N], preferred_element_type=F32)
        u = jnp.dot(h, wgu_ref[:, d_ff + lo:d_ff + lo + V7X_MXU_N], preferred_element_type=F32)
        act = g * jax.nn.sigmoid(g) * u
        k = j - FFN_FIRST_FINISH_CHUNK
        if finish_rows is not None and 0 <= k < FFN_FINISH_BLOCKS:
            act = _after(act, finish_rows(k))
        a_ref[:, lo:lo + V7X_MXU_N] = act.astype(BF16)
    y_ref[...] = jnp.dot(a_ref[...], wd_ref[...], preferred_element_type=F32)
    xs_ref[...] = x


def _ffn_body(layer, half, sub, d_ff, x_ref, mod_ref, mod_prev_ref, wgu_hbm, wd_hbm, lng_ref, lnb_ref, o_ref,
              wgu_ref, wd_ref, stage_gu_ref, stage_d_ref, sem_ref, a_ref, y_ref, xs_ref):
    i = pl.program_id(0)
    last = pl.num_programs(0) - 1
    n_chunks = d_ff // V7X_MXU_N
    start = functools.partial(_ffn_start, sub, d_ff, x_ref, mod_ref, wgu_ref, wd_ref, a_ref, y_ref, xs_ref)
    finish_rows = functools.partial(_finish_rows, sub, 0.5, FFN_FINISH_BLOCKS, mod_prev_ref, lng_ref, lnb_ref,
                                    y_ref, xs_ref, o_ref)
    copies = functools.partial(_ffn_weight_copies, layer, half, d_ff, wgu_hbm, wd_hbm, stage_gu_ref, stage_d_ref,
                               sem_ref)
    take_chunk = functools.partial(_ffn_take_chunk, copies, n_chunks, d_ff, wgu_ref, wd_ref, stage_gu_ref,
                                   stage_d_ref)

    @pl.when(i == 0)
    def _():
        for j in range(FFN_WEIGHT_SLOTS):
            for copy in copies(j):
                copy.start()
        start(take_chunk=take_chunk)

    @pl.when((i > 0) & (i < last))
    def _():
        start(finish_rows=finish_rows)

    @pl.when(i == last)
    def _():
        for k in range(FFN_FINISH_BLOCKS):
            finish_rows(k)


def _ffn(x, mod, w_gu, w_down, ln_g, ln_b, *, layer, sub, half, seq, tm):
    m, d = x.shape
    d_ff = w_down.shape[2]
    n_tiles = m // tm
    in_specs, out_spec, parking = _pipelined_specs(n_tiles, seq // tm, tm, d, layer)
    return pl.pallas_call(
        functools.partial(_ffn_body, layer, half, sub, d_ff),
        grid=(n_tiles + 1,),
        in_specs=in_specs + [
            pl.BlockSpec(memory_space=pl.ANY),
            pl.BlockSpec(memory_space=pl.ANY),
            _resident((None, 3, d), lambda i: (layer, 0, 0)),
            _resident((None, 3, d), lambda i: (layer, 0, 0)),
        ],
        out_specs=out_spec,
        out_shape=jax.ShapeDtypeStruct((m, d), F32),
        scratch_shapes=[
            pltpu.VMEM((d, 2 * d_ff), BF16),
            pltpu.VMEM((d_ff, d), BF16),
            pltpu.VMEM((FFN_WEIGHT_SLOTS, 2, d, V7X_MXU_N), F32),
            pltpu.VMEM((FFN_WEIGHT_SLOTS, V7X_MXU_N, d), F32),
            pltpu.SemaphoreType.DMA((FFN_WEIGHT_SLOTS, 3)),
            pltpu.VMEM((tm, d_ff), BF16),
        ] + parking,
        compiler_params=_params(1),
        name="ffn",
    )(x, mod, mod, w_gu, w_down, ln_g, ln_b)


def _gmlp_body(x_ref, mod_ref, win_ref, bin_ref, glng_ref, glnb_ref, ws_ref, bst_ref, wout_ref,
               lng_ref, lnb_ref, o_ref, v_ref, m_ref):
    tm = x_ref.shape[0]
    half = wout_ref.shape[0]
    gdim = half // GMLP_GROUPS
    x = x_ref[...]
    h = _modulate(x, mod_ref, 1).astype(BF16)

    def project(lo):
        return jnp.dot(h, win_ref[:, lo:lo + gdim], preferred_element_type=F32) + bin_ref[:, lo:lo + gdim]

    order = [half + g * gdim for g in range(GMLP_GROUPS)] + [g * gdim for g in range(GMLP_GROUPS)]
    pending = [project(lo) for lo in order[:GMLP_DOTS_AHEAD]]

    def next_projection(n):
        z = pending.pop(0)
        if n + GMLP_DOTS_AHEAD < len(order):
            pending.append(project(order[n + GMLP_DOTS_AHEAD]))
        return z

    for g in range(GMLP_GROUPS):
        v_ref[:, g * gdim:(g + 1) * gdim] = _gelu(next_projection(g))
    v = v_ref[...]
    mu = jnp.mean(v, axis=-1, keepdims=True)
    dv = v - mu
    rstd = lax.rsqrt(jnp.mean(dv * dv, axis=-1, keepdims=True) + LN_EPS)

    t_out = lax.broadcasted_iota(jnp.int32, (GMLP_WINDOW, GMLP_WINDOW), 0)
    s_in = lax.broadcasted_iota(jnp.int32, (GMLP_WINDOW, GMLP_WINDOW), 1)
    causal = (s_in // CHUNK) <= (t_out // CHUNK)

    for g in range(GMLP_GROUPS):
        lo = g * gdim
        vn = ((v_ref[:, lo:lo + gdim] - mu) * rstd * glng_ref[:, lo:lo + gdim]
              + glnb_ref[:, lo:lo + gdim]).astype(BF16)
        ws_g = jnp.where(causal, ws_ref[g], 0.0).astype(BF16)
        b_col = bst_ref[:, g:g + 1]
        u = _gelu(next_projection(GMLP_GROUPS + g))
        for w in range(tm // GMLP_WINDOW):
            rows = slice(w * GMLP_WINDOW, (w + 1) * GMLP_WINDOW)
            s = jnp.dot(ws_g, vn[rows], preferred_element_type=F32) + b_col
            m_ref[rows, lo:lo + gdim] = (u[rows] * s).astype(BF16)

    y = jnp.dot(m_ref[...], wout_ref[...], preferred_element_type=F32)
    r = ALPHA * x + _gate(mod_ref, 1) * y
    o_ref[...] = _layer_norm(r, lng_ref[1:2, :], lnb_ref[1:2, :])


def _gmlp(x, mod, w_in, b_in, gln_g, gln_b, w_s, b_s_t, w_out, ln_g, ln_b, *, layer, seq, tm):
    m, d = x.shape
    width = w_in.shape[2]
    half = width // 2
    tiles_per_seq = seq // tm
    const = lambda i: (layer, 0, 0)
    return pl.pallas_call(
        _gmlp_body,
        grid=(m // tm,),
        in_specs=[
            pl.BlockSpec((tm, d), lambda i: (i, 0)),
            pl.BlockSpec((None, None, N_MOD, d), lambda i: (layer, i // tiles_per_seq, 0, 0)),
            _resident((None, d, width), const),
            _resident((None, 1, width), const),
            _resident((None, 1, half), const),
            _resident((None, 1, half), const),
            _resident((None, GMLP_GROUPS, GMLP_WINDOW, GMLP_WINDOW), lambda i: (layer, 0, 0, 0)),
            _resident((None, GMLP_WINDOW, GMLP_GROUPS), const),
            _resident((None, half, d), const),
            _resident((None, 3, d), const),
            _resident((None, 3, d), const),
        ],
        out_specs=pl.BlockSpec((tm, d), lambda i: (i, 0)),
        out_shape=jax.ShapeDtypeStruct((m, d), F32),
        scratch_shapes=[pltpu.VMEM((tm, half), F32), pltpu.VMEM((tm, half), BF16)],
        compiler_params=_params(1),
        name="gmlp",
    )(x, mod, w_in, b_in, gln_g, gln_b, w_s, b_s_t, w_out, ln_g, ln_b)


def _kv_body(x_ref, mkv_ref, wk_ref, wvt_ref, k_ref, vt_ref):
    i = pl.program_id(1)

    @pl.when(i == 0)
    def _():
        k_ref[...] = jnp.zeros_like(k_ref)
        vt_ref[...] = jnp.zeros_like(vt_ref)

    @pl.when(i > 0)
    def _():
        h = (x_ref[...] * (1.0 + mkv_ref[1:2, :]) + mkv_ref[0:1, :]).astype(BF16)
        k = jnp.dot(h, wk_ref[...], preferred_element_type=F32)
        k_ref[...] = k.astype(BF16).reshape(k_ref.shape)
        vt = lax.dot_general(wvt_ref[...], h, (((1,), (1,)), ((), ())), preferred_element_type=F32)
        for n in range(vt_ref.shape[0]):
            vt_ref[n] = vt[:, n * KV_BLOCK:(n + 1) * KV_BLOCK].astype(BF16)


def _kv_project(x, mkv, w_k, w_v_t):
    bsz, seq, d = x.shape
    tm = LEFT_PAD
    blocks_per_tile = tm // KV_BLOCK
    n_blocks = (LEFT_PAD + seq) // KV_BLOCK
    return pl.pallas_call(
        _kv_body,
        grid=(bsz, 1 + seq // tm),
        in_specs=[
            pl.BlockSpec((None, tm, d), lambda b, i: (b, jnp.maximum(i - 1, 0), 0)),
            pl.BlockSpec((None, 2, d), lambda b, i: (b, 0, 0)),
            _resident((d, d), lambda b, i: (0, 0)),
            _resident((d, d), lambda b, i: (0, 0)),
        ],
        out_specs=[
            pl.BlockSpec((None, blocks_per_tile, KV_BLOCK, d), lambda b, i: (b, i, 0, 0)),
            pl.BlockSpec((None, blocks_per_tile, d, KV_BLOCK), lambda b, i: (b, i, 0, 0)),
        ],
        out_shape=[
            jax.ShapeDtypeStruct((bsz, n_blocks, KV_BLOCK, d), BF16),
            jax.ShapeDtypeStruct((bsz, n_blocks, d, KV_BLOCK), BF16),
        ],
        compiler_params=_params(2),
        name="kv_project",
    )(x, mkv, w_k, w_v_t)


def _bias_body(rb_ref, o_ref):
    rows, tq = o_ref.shape[1], o_ref.shape[2]
    width = 2 * tq
    r0 = pl.program_id(0) * rows
    rb = rb_ref[...]
    hi = rb.astype(BF16)
    rem = rb - hi.astype(F32)
    mid = rem.astype(BF16)
    lo = (rem - mid.astype(F32)).astype(BF16)
    lane = lax.broadcasted_iota(jnp.int32, (1, width), 1)
    rel = lax.broadcasted_iota(jnp.int32, (N_REL, width), 0)
    idx = jnp.clip(lane - (rows - 1) - r0 + LEFT_PAD, -(CHUNK - 1), MAX_REL) + (CHUNK - 1)
    onehot = (rel == idx).astype(BF16)
    wide = (jnp.dot(hi, onehot, preferred_element_type=F32)
            + jnp.dot(mid, onehot, preferred_element_type=F32)
            + jnp.dot(lo, onehot, preferred_element_type=F32)) * LOG2E
    qc = lax.broadcasted_iota(jnp.int32, (1, tq), 1) // CHUNK
    for k in range(rows):
        kc = (r0 + k) // CHUNK
        visible = (kc >= qc) & (kc <= qc + LEFT_CHUNKS)
        o_ref[:, k, :] = jnp.where(visible, wide[:, rows - 1 - k:rows - 1 - k + tq], NEG_BIG)


def _bias_table(rel_bias, tq):
    nh = rel_bias.shape[0]
    tk = tq + LEFT_PAD
    rows = BIAS_ROWS_PER_STEP
    assert tq + rows - 1 <= 2 * tq and tk % rows == 0
    return pl.pallas_call(
        _bias_body,
        grid=(tk // rows,),
        in_specs=[pl.BlockSpec((nh, N_REL), lambda i: (0, 0))],
        out_specs=pl.BlockSpec((nh, rows, tq), lambda i: (0, i, 0)),
        out_shape=jax.ShapeDtypeStruct((nh, tk, tq), F32),
        compiler_params=_params(1),
        name="bias_table",
    )(rel_bias)


def _attn_step(mask_padding, x_ref, mod_ref, wq_ref, wo_ref, k_ref, vt_ref, bias_ref, lng_ref, lnb_ref,
               o_ref, ctxt_ref):
    tq, d = x_ref.shape
    win_blocks = (tq + LEFT_PAD) // KV_BLOCK
    tk = win_blocks * KV_BLOCK
    sub_k = ATTN_SUB_Q + LEFT_PAD
    hd = d // N_HEADS
    i = pl.program_id(1)
    x = x_ref[...]
    h = _modulate(x, mod_ref, 1)
    q = jnp.dot(h.astype(BF16), wq_ref[...], preferred_element_type=F32) * (hd ** -0.5 * LOG2E)
    q = q.astype(BF16)

    first_key_row = LEFT_PAD - i * tq
    first_head = lax.broadcasted_iota(jnp.int32, (tq, V7X_LANES), 1) < hd
    ones_rows = jnp.ones((V7X_BF16_SUBLANES, KV_BLOCK), BF16)

    def scores_t(head):
        lanes = slice(head // 2 * V7X_LANES, (head // 2 + 1) * V7X_LANES)
        q_pair = q[:, lanes]
        k_win = k_ref[pl.ds(i, win_blocks), :, lanes].reshape(tk, V7X_LANES)
        q_one = jnp.where(first_head if head % 2 == 0 else ~first_head, q_pair, jnp.zeros_like(q_pair))
        return lax.dot_general(k_win, q_one, (((1,), (1,)), ((), ())), preferred_element_type=F32)

    def softmax_t(head, s_t):
        probs = []
        for r0 in range(0, tq, ATTN_SUB_Q):
            s_sub = s_t[r0:r0 + sub_k, r0:r0 + ATTN_SUB_Q] + bias_ref[head]
            if mask_padding:
                row = lax.broadcasted_iota(jnp.int32, (sub_k, 1), 0)
                s_sub = jnp.where(row >= first_key_row - r0, s_sub, NEG_BIG)
            s_max = jnp.max(s_sub, axis=0, keepdims=True)
            pieces = [jnp.exp2(s_sub - s_max).astype(BF16)]
            if r0:
                pieces.insert(0, jnp.zeros((r0, ATTN_SUB_Q), BF16))
            if tk - sub_k - r0:
                pieces.append(jnp.zeros((tk - sub_k - r0, ATTN_SUB_Q), BF16))
            probs.append(jnp.concatenate(pieces, axis=0))
        return jnp.concatenate(probs, axis=1)

    def context_t(head, p_t):
        rows = slice(head * hd, (head + 1) * hd)
        acc = None
        for n in range(win_blocks):
            v_t = vt_ref[pl.ds(i + n, 1), rows, :].reshape(hd, KV_BLOCK)
            lhs = jnp.concatenate([v_t, ones_rows], axis=0)
            part = jnp.dot(lhs, p_t[n * KV_BLOCK:(n + 1) * KV_BLOCK], preferred_element_type=F32)
            acc = part if acc is None else acc + part
        ctxt_ref[rows, :] = (acc[:hd] / acc[hd:hd + 1]).astype(BF16)

    pending = [scores_t(head) for head in range(ATTN_HEADS_AHEAD)]
    for head in range(N_HEADS):
        s_cur = pending.pop(0)
        if head + ATTN_HEADS_AHEAD < N_HEADS:
            pending.append(scores_t(head + ATTN_HEADS_AHEAD))
        context_t(head, softmax_t(head, s_cur))

    y = lax.dot_general(ctxt_ref[...], wo_ref[...], (((0,), (0,)), ((), ())), preferred_element_type=F32)
    r = ALPHA * x + _gate(mod_ref, 1) * y
    o_ref[...] = _layer_norm(r, lng_ref[1:2, :], lnb_ref[1:2, :])


def _attn_body(x_ref, *refs):
    has_padding = pl.program_id(1) * x_ref.shape[0] < LEFT_PAD
    pl.when(has_padding)(functools.partial(_attn_step, True, x_ref, *refs))
    pl.when(jnp.logical_not(has_padding))(functools.partial(_attn_step, False, x_ref, *refs))


def _attention(x, mod, w_q, w_o, k_pad, vt_pad, bias, ln_g, ln_b, *, layer, j):
    bsz, seq, d = x.shape
    tq = KV_BLOCK
    kv_blocks = k_pad.shape[1]
    return pl.pallas_call(
        _attn_body,
        grid=(bsz, seq // tq),
        in_specs=[
            pl.BlockSpec((None, tq, d), lambda b, i: (b, i, 0)),
            pl.BlockSpec((None, None, N_MOD, d), lambda b, i: (layer, b, 0, 0)),
            _resident((None, d, d), lambda b, i: (j, 0, 0)),
            _resident((None, d, d), lambda b, i: (j, 0, 0)),
            pl.BlockSpec((None, kv_blocks, KV_BLOCK, d), lambda b, i: (b, 0, 0, 0)),
            pl.BlockSpec((None, kv_blocks, d, KV_BLOCK), lambda b, i: (b, 0, 0, 0)),
            _resident((N_HEADS, ATTN_SUB_Q + LEFT_PAD, ATTN_SUB_Q), lambda b, i: (j, 0, 0)),
            _resident((None, 3, d), lambda b, i: (layer, 0, 0)),
            _resident((None, 3, d), lambda b, i: (layer, 0, 0)),
        ],
        out_specs=pl.BlockSpec((None, tq, d), lambda b, i: (b, i, 0)),
        out_shape=jax.ShapeDtypeStruct((bsz, seq, d), F32),
        scratch_shapes=[pltpu.VMEM((d, tq), BF16)],
        compiler_params=_params(2),
        name="attention",
    )(x, mod, w_q, w_o, k_pad, vt_pad, bias, ln_g, ln_b)


def kernel(x, c, w_ada, b_ada, ln_g, ln_b, ffn_gu, ffn_down, gmlp_w_in, gmlp_b_in, gmlp_ln_g, gmlp_ln_b,
           gmlp_w_s, gmlp_b_s, gmlp_w_out, w_ada_kv, b_ada_kv, w_kv, attn_w_q, attn_rel_bias, attn_w_o):
    bsz, seq, d = x.shape
    mod = _adaln(c, w_ada, b_ada, ADALN_TN).reshape(DEPTH, bsz, N_MOD, d)
    mkv = _adaln(c, w_ada_kv[None], b_ada_kv[None], ADALN_TN).reshape(bsz, 2, d)

    w_in_b = gmlp_w_in.astype(BF16)
    w_out_b = gmlp_w_out.astype(BF16)
    w_k_b = w_kv[:, :d].astype(BF16)
    w_v_t_b = w_kv[:, d:].T.astype(BF16)
    w_q_b = attn_w_q.astype(BF16)
    w_o_b = attn_w_o.astype(BF16)
    b_in = gmlp_b_in[:, None, :]
    gln_g = gmlp_ln_g[:, None, :]
    gln_b = gmlp_ln_b[:, None, :]
    b_s_t = jnp.swapaxes(gmlp_b_s, 1, 2)
    bias = _bias_table(attn_rel_bias.reshape(N_B * N_HEADS, N_REL), ATTN_SUB_Q)

    xf = x.reshape(bsz * seq, d)
    k_pad = vt_pad = None
    for l in range(DEPTH):
        ffn = functools.partial(_ffn, mod=mod, w_gu=ffn_gu, w_down=ffn_down, ln_g=ln_g, ln_b=ln_b,
                                layer=l, seq=seq, tm=FFN_TM)
        xf = ffn(xf, sub=0, half=0)
        if l < N_A:
            xf = _gmlp(xf, mod, w_in_b, b_in, gln_g, gln_b, gmlp_w_s, b_s_t, w_out_b, ln_g, ln_b,
                       layer=l, seq=seq, tm=GMLP_TM)
        else:
            xf = _attention(xf.reshape(bsz, seq, d), mod, w_q_b, w_o_b, k_pad, vt_pad, bias, ln_g, ln_b,
                            layer=l, j=l - N_A).reshape(bsz * seq, d)
        xf = ffn(xf, sub=2, half=1)
        if l == N_A - 1:
            k_pad, vt_pad = _kv_project(xf.reshape(bsz, seq, d), mkv, w_k_b, w_v_t_b)
    return xf.reshape(bsz, seq, d)
```

```python
import functools

import jax
import jax.numpy as jnp
from jax import lax
from jax.experimental import pallas as pl
from jax.experimental.pallas import tpu as pltpu

DEPTH = 4
CHUNK = 64
N_A = DEPTH // 2
N_B = DEPTH - N_A
GMLP_WINDOW = 128
GMLP_GROUPS = 8
N_HEADS = 16
LEFT_CHUNKS = 8
LEFT_PAD = LEFT_CHUNKS * CHUNK
MAX_REL = 4 * CHUNK
N_REL = (CHUNK - 1) + MAX_REL + 1
ALPHA = (2.0 * DEPTH) ** 0.25
LN_EPS = 1e-5
N_MOD = 9

V7X_LANES = 128
V7X_MXU_N = 256
V7X_BF16_SUBLANES = 16
V7X_VMEM_LIMIT_BYTES = 56 * 1024 * 1024

NEG_BIG = -1e30
LOG2E = 1.4426950408889634

KV_BLOCK = 256
ATTN_TQ = 512
ATTN_SUB_Q = 128
ATTN_HEADS_AHEAD = 3

FFN_TM = 512
FFN_FINISH_BLOCKS = 8
FFN_FIRST_FINISH_CHUNK = 1
FFN_WEIGHT_SLOTS = 3
GMLP_TM = 512
GMLP_DOTS_AHEAD = 3
ADALN_TN = 1024
BIAS_ROWS_PER_STEP = 64

F32 = jnp.float32
BF16 = jnp.bfloat16


def _params(n_axes):
    return pltpu.CompilerParams(
        dimension_semantics=("arbitrary",) * n_axes,
        vmem_limit_bytes=V7X_VMEM_LIMIT_BYTES,
    )


def _resident(block_shape, index_map):
    return pl.BlockSpec(block_shape, index_map, pipeline_mode=pl.Buffered(1))


def _layer_norm(r, g, b):
    mu = jnp.mean(r, axis=-1, keepdims=True)
    d = r - mu
    var = jnp.mean(d * d, axis=-1, keepdims=True)
    return d * lax.rsqrt(var + LN_EPS) * g + b


def _gelu(z):
    return 0.5 * z * (1.0 + lax.erf(z * (0.5 ** 0.5)))


def _modulate(x, mod_ref, sub):
    shift = mod_ref[3 * sub:3 * sub + 1, :]
    scale = mod_ref[3 * sub + 1:3 * sub + 2, :]
    return x * (1.0 + scale) + shift


def _gate(mod_ref, sub):
    return 1.0 + mod_ref[3 * sub + 2:3 * sub + 3, :]


def _zero_bits_of(v):
    bits = lax.bitcast_convert_type(v, jnp.int32)
    sixteen = jnp.full(bits.shape, 16, jnp.int32)
    cleared = lax.shift_right_logical(lax.shift_right_logical(bits, sixteen), sixteen)
    return jnp.max(cleared, axis=0, keepdims=True)


def _after(v, finished_rows):
    zero = _zero_bits_of(finished_rows)[:, :v.shape[1]]
    return jnp.where(zero == 0, v, jnp.zeros_like(v))


def _finish_rows(sub, branch_scale, n_blocks, mod_prev_ref, lng_ref, lnb_ref, y_ref, xs_ref, o_ref, k):
    n = o_ref.shape[0] // n_blocks
    rows = slice(k * n, (k + 1) * n)
    r = ALPHA * xs_ref[rows, :] + (branch_scale * _gate(mod_prev_ref, sub)) * y_ref[rows, :]
    out = _layer_norm(r, lng_ref[sub:sub + 1, :], lnb_ref[sub:sub + 1, :])
    o_ref[rows, :] = out
    return out


def _pipelined_specs(n_tiles, tiles_per_seq, tm, d, layer):
    cur = lambda i: jnp.minimum(i, n_tiles - 1)
    prev = lambda i: jnp.maximum(i - 1, 0)
    in_specs = [
        pl.BlockSpec((tm, d), lambda i: (cur(i), 0)),
        pl.BlockSpec((None, None, N_MOD, d), lambda i: (layer, cur(i) // tiles_per_seq, 0, 0)),
        pl.BlockSpec((None, None, N_MOD, d), lambda i: (layer, prev(i) // tiles_per_seq, 0, 0)),
    ]
    out_spec = pl.BlockSpec((tm, d), lambda i: (prev(i), 0))
    parking = [pltpu.VMEM((tm, d), F32), pltpu.VMEM((tm, d), F32)]
    return in_specs, out_spec, parking


def _adaln_body(c_ref, w_ref, b_ref, o_ref):
    c = c_ref[...]
    c_act = c * jax.nn.sigmoid(c)
    o_ref[...] = jnp.dot(c_act, w_ref[...], preferred_element_type=F32) + b_ref[...]


def _adaln(c, w, b, tn):
    n_l, d, n = w.shape
    bsz = c.shape[0]
    return pl.pallas_call(
        _adaln_body,
        grid=(n_l, n // tn),
        in_specs=[
            pl.BlockSpec((bsz, d), lambda l, j: (0, 0)),
            pl.BlockSpec((None, d, tn), lambda l, j: (l, 0, j)),
            pl.BlockSpec((None, 1, tn), lambda l, j: (l, 0, j)),
        ],
        out_specs=pl.BlockSpec((None, bsz, tn), lambda l, j: (l, 0, j)),
        out_shape=jax.ShapeDtypeStruct((n_l, bsz, n), F32),
        compiler_params=_params(2),
        name="adaln",
    )(c, w, b.reshape(n_l, 1, n))


def _ffn_weight_copies(layer, half, d_ff, wgu_hbm, wd_hbm, stage_gu_ref, stage_d_ref, sem_ref, j):
    lo = j * V7X_MXU_N
    slot = j % FFN_WEIGHT_SLOTS
    return (
        pltpu.make_async_copy(wgu_hbm.at[layer, half, :, pl.ds(lo, V7X_MXU_N)],
                              stage_gu_ref.at[slot, 0], sem_ref.at[slot, 0]),
        pltpu.make_async_copy(wgu_hbm.at[layer, half, :, pl.ds(d_ff + lo, V7X_MXU_N)],
                              stage_gu_ref.at[slot, 1], sem_ref.at[slot, 1]),
        pltpu.make_async_copy(wd_hbm.at[layer, half, pl.ds(lo, V7X_MXU_N), :],
                              stage_d_ref.at[slot], sem_ref.at[slot, 2]),
    )


def _ffn_take_chunk(copies, n_chunks, d_ff, wgu_ref, wd_ref, stage_gu_ref, stage_d_ref, j):
    for copy in copies(j):
        copy.wait()
    lo = j * V7X_MXU_N
    slot = j % FFN_WEIGHT_SLOTS
    wgu_ref[:, lo:lo + V7X_MXU_N] = stage_gu_ref[slot, 0].astype(BF16)
    wgu_ref[:, d_ff + lo:d_ff + lo + V7X_MXU_N] = stage_gu_ref[slot, 1].astype(BF16)
    wd_ref[lo:lo + V7X_MXU_N, :] = stage_d_ref[slot].astype(BF16)
    if j + FFN_WEIGHT_SLOTS < n_chunks:
        for copy in copies(j + FFN_WEIGHT_SLOTS):
            copy.start()


def _ffn_start(sub, d_ff, x_ref, mod_ref, wgu_ref, wd_ref, a_ref, y_ref, xs_ref, finish_rows=None, take_chunk=None):
    x = x_ref[...]
    h = _modulate(x, mod_ref, sub).astype(BF16)
    for j in range(d_ff // V7X_MXU_N):
        lo = j * V7X_MXU_N
        if take_chunk is not None:
            take_chunk(j)
        g = jnp.dot(h, wgu_ref[:, lo:lo + V7X_MXU_N], preferred_element_type=F32)
        u = jnp.dot(h, wgu_ref[:, d_ff + lo:d_ff + lo + V7X_MXU_N], preferred_element_type=F32)
        act = g * jax.nn.sigmoid(g) * u
        k = j - FFN_FIRST_FINISH_CHUNK
        if finish_rows is not None and 0 <= k < FFN_FINISH_BLOCKS:
            act = _after(act, finish_rows(k))
        a_ref[:, lo:lo + V7X_MXU_N] = act.astype(BF16)
    y_ref[...] = jnp.dot(a_ref[...], wd_ref[...], preferred_element_type=F32)
    xs_ref[...] = x


def _ffn_body(layer, half, sub, d_ff, x_ref, mod_ref, mod_prev_ref, wgu_hbm, wd_hbm, lng_ref, lnb_ref, o_ref,
              wgu_ref, wd_ref, stage_gu_ref, stage_d_ref, sem_ref, a_ref, y_ref, xs_ref):
    i = pl.program_id(0)
    last = pl.num_programs(0) - 1
    n_chunks = d_ff // V7X_MXU_N
    start = functools.partial(_ffn_start, sub, d_ff, x_ref, mod_ref, wgu_ref, wd_ref, a_ref, y_ref, xs_ref)
    finish_rows = functools.partial(_finish_rows, sub, 0.5, FFN_FINISH_BLOCKS, mod_prev_ref, lng_ref, lnb_ref,
                                    y_ref, xs_ref, o_ref)
    copies = functools.partial(_ffn_weight_copies, layer, half, d_ff, wgu_hbm, wd_hbm, stage_gu_ref, stage_d_ref,
                               sem_ref)
    take_chunk = functools.partial(_ffn_take_chunk, copies, n_chunks, d_ff, wgu_ref, wd_ref, stage_gu_ref,
                                   stage_d_ref)

    @pl.when(i == 0)
    def _():
        for j in range(FFN_WEIGHT_SLOTS):
            for copy in copies(j):
                copy.start()
        start(take_chunk=take_chunk)

    @pl.when((i > 0) & (i < last))
    def _():
        start(finish_rows=finish_rows)

    @pl.when(i == last)
    def _():
        for k in range(FFN_FINISH_BLOCKS):
            finish_rows(k)


def _ffn(x, mod, w_gu, w_down, ln_g, ln_b, *, layer, sub, half, seq, tm):
    m, d = x.shape
    d_ff = w_down.shape[2]
    n_tiles = m // tm
    in_specs, out_spec, parking = _pipelined_specs(n_tiles, seq // tm, tm, d, layer)
    return pl.pallas_call(
        functools.partial(_ffn_body, layer, half, sub, d_ff),
        grid=(n_tiles + 1,),
        in_specs=in_specs + [
            pl.BlockSpec(memory_space=pl.ANY),
            pl.BlockSpec(memory_space=pl.ANY),
            _resident((None, 3, d), lambda i: (layer, 0, 0)),
            _resident((None, 3, d), lambda i: (layer, 0, 0)),
        ],
        out_specs=out_spec,
        out_shape=jax.ShapeDtypeStruct((m, d), F32),
        scratch_shapes=[
            pltpu.VMEM((d, 2 * d_ff), BF16),
            pltpu.VMEM((d_ff, d), BF16),
            pltpu.VMEM((FFN_WEIGHT_SLOTS, 2, d, V7X_MXU_N), F32),
            pltpu.VMEM((FFN_WEIGHT_SLOTS, V7X_MXU_N, d), F32),
            pltpu.SemaphoreType.DMA((FFN_WEIGHT_SLOTS, 3)),
            pltpu.VMEM((tm, d_ff), BF16),
        ] + parking,
        compiler_params=_params(1),
        name="ffn",
    )(x, mod, mod, w_gu, w_down, ln_g, ln_b)


def _gmlp_body(x_ref, mod_ref, win_ref, bin_ref, glng_ref, glnb_ref, ws_ref, bst_ref, wout_ref,
               lng_ref, lnb_ref, o_ref, v_ref, m_ref):
    tm = x_ref.shape[0]
    half = wout_ref.shape[0]
    gdim = half // GMLP_GROUPS
    x = x_ref[...]
    h = _modulate(x, mod_ref, 1).astype(BF16)

    def project(lo):
        return jnp.dot(h, win_ref[:, lo:lo + gdim], preferred_element_type=F32) + bin_ref[:, lo:lo + gdim]

    order = [half + g * gdim for g in range(GMLP_GROUPS)] + [g * gdim for g in range(GMLP_GROUPS)]
    pending = [project(lo) for lo in order[:GMLP_DOTS_AHEAD]]

    def next_projection(n):
        z = pending.pop(0)
        if n + GMLP_DOTS_AHEAD < len(order):
            pending.append(project(order[n + GMLP_DOTS_AHEAD]))
        return z

    for g in range(GMLP_GROUPS):
        v_ref[:, g * gdim:(g + 1) * gdim] = _gelu(next_projection(g))
    v = v_ref[...]
    mu = jnp.mean(v, axis=-1, keepdims=True)
    dv = v - mu
    rstd = lax.rsqrt(jnp.mean(dv * dv, axis=-1, keepdims=True) + LN_EPS)

    t_out = lax.broadcasted_iota(jnp.int32, (GMLP_WINDOW, GMLP_WINDOW), 0)
    s_in = lax.broadcasted_iota(jnp.int32, (GMLP_WINDOW, GMLP_WINDOW), 1)
    causal = (s_in // CHUNK) <= (t_out // CHUNK)

    for g in range(GMLP_GROUPS):
        lo = g * gdim
        vn = ((v_ref[:, lo:lo + gdim] - mu) * rstd * glng_ref[:, lo:lo + gdim]
              + glnb_ref[:, lo:lo + gdim]).astype(BF16)
        ws_g = jnp.where(causal, ws_ref[g], 0.0).astype(BF16)
        b_col = bst_ref[:, g:g + 1]
        u = _gelu(next_projection(GMLP_GROUPS + g))
        for w in range(tm // GMLP_WINDOW):
            rows = slice(w * GMLP_WINDOW, (w + 1) * GMLP_WINDOW)
            s = jnp.dot(ws_g, vn[rows], preferred_element_type=F32) + b_col
            m_ref[rows, lo:lo + gdim] = (u[rows] * s).astype(BF16)

    y = jnp.dot(m_ref[...], wout_ref[...], preferred_element_type=F32)
    r = ALPHA * x + _gate(mod_ref, 1) * y
    o_ref[...] = _layer_norm(r, lng_ref[1:2, :], lnb_ref[1:2, :])


def _gmlp(x, mod, w_in, b_in, gln_g, gln_b, w_s, b_s_t, w_out, ln_g, ln_b, *, layer, seq, tm):
    m, d = x.shape
    width = w_in.shape[2]
    half = width // 2
    tiles_per_seq = seq // tm
    const = lambda i: (layer, 0, 0)
    return pl.pallas_call(
        _gmlp_body,
        grid=(m // tm,),
        in_specs=[
            pl.BlockSpec((tm, d), lambda i: (i, 0)),
            pl.BlockSpec((None, None, N_MOD, d), lambda i: (layer, i // tiles_per_seq, 0, 0)),
            _resident((None, d, width), const),
            _resident((None, 1, width), const),
            _resident((None, 1, half), const),
            _resident((None, 1, half), const),
            _resident((None, GMLP_GROUPS, GMLP_WINDOW, GMLP_WINDOW), lambda i: (layer, 0, 0, 0)),
            _resident((None, GMLP_WINDOW, GMLP_GROUPS), const),
            _resident((None, half, d), const),
            _resident((None, 3, d), const),
            _resident((None, 3, d), const),
        ],
        out_specs=pl.BlockSpec((tm, d), lambda i: (i, 0)),
        out_shape=jax.ShapeDtypeStruct((m, d), F32),
        scratch_shapes=[pltpu.VMEM((tm, half), F32), pltpu.VMEM((tm, half), BF16)],
        compiler_params=_params(1),
        name="gmlp",
    )(x, mod, w_in, b_in, gln_g, gln_b, w_s, b_s_t, w_out, ln_g, ln_b)


def _kv_body(x_ref, mkv_ref, wk_ref, wvt_ref, k_ref, vt_ref):
    i = pl.program_id(1)

    @pl.when(i == 0)
    def _():
        k_ref[...] = jnp.zeros_like(k_ref)
        vt_ref[...] = jnp.zeros_like(vt_ref)

    @pl.when(i > 0)
    def _():
        h = (x_ref[...] * (1.0 + mkv_ref[1:2, :]) + mkv_ref[0:1, :]).astype(BF16)
        k = jnp.dot(h, wk_ref[...], preferred_element_type=F32)
        k_ref[...] = k.astype(BF16).reshape(k_ref.shape)
        vt = lax.dot_general(wvt_ref[...], h, (((1,), (1,)), ((), ())), preferred_element_type=F32)
        for n in range(vt_ref.shape[0]):
            vt_ref[n] = vt[:, n * KV_BLOCK:(n + 1) * KV_BLOCK].astype(BF16)


def _kv_project(x, mkv, w_k, w_v_t):
    bsz, seq, d = x.shape
    tm = LEFT_PAD
    blocks_per_tile = tm // KV_BLOCK
    n_blocks = (LEFT_PAD + seq) // KV_BLOCK
    return pl.pallas_call(
        _kv_body,
        grid=(bsz, 1 + seq // tm),
        in_specs=[
            pl.BlockSpec((None, tm, d), lambda b, i: (b, jnp.maximum(i - 1, 0), 0)),
            pl.BlockSpec((None, 2, d), lambda b, i: (b, 0, 0)),
            _resident((d, d), lambda b, i: (0, 0)),
            _resident((d, d), lambda b, i: (0, 0)),
        ],
        out_specs=[
            pl.BlockSpec((None, blocks_per_tile, KV_BLOCK, d), lambda b, i: (b, i, 0, 0)),
            pl.BlockSpec((None, blocks_per_tile, d, KV_BLOCK), lambda b, i: (b, i, 0, 0)),
        ],
        out_shape=[
            jax.ShapeDtypeStruct((bsz, n_blocks, KV_BLOCK, d), BF16),
            jax.ShapeDtypeStruct((bsz, n_blocks, d, KV_BLOCK), BF16),
        ],
        compiler_params=_params(2),
        name="kv_project",
    )(x, mkv, w_k, w_v_t)


def _bias_body(rb_ref, o_ref):
    rows, tq = o_ref.shape[1], o_ref.shape[2]
    width = 2 * tq
    r0 = pl.program_id(0) * rows
    rb = rb_ref[...]
    hi = rb.astype(BF16)
    rem = rb - hi.astype(F32)
    mid = rem.astype(BF16)
    lo = (rem - mid.astype(F32)).astype(BF16)
    lane = lax.broadcasted_iota(jnp.int32, (1, width), 1)
    rel = lax.broadcasted_iota(jnp.int32, (N_REL, width), 0)
    idx = jnp.clip(lane - (rows - 1) - r0 + LEFT_PAD, -(CHUNK - 1), MAX_REL) + (CHUNK - 1)
    onehot = (rel == idx).astype(BF16)
    wide = (jnp.dot(hi, onehot, preferred_element_type=F32)
            + jnp.dot(mid, onehot, preferred_element_type=F32)
            + jnp.dot(lo, onehot, preferred_element_type=F32)) * LOG2E
    qc = lax.broadcasted_iota(jnp.int32, (1, tq), 1) // CHUNK
    for k in range(rows):
        kc = (r0 + k) // CHUNK
        visible = (kc >= qc) & (kc <= qc + LEFT_CHUNKS)
        o_ref[:, k, :] = jnp.where(visible, wide[:, rows - 1 - k:rows - 1 - k + tq], NEG_BIG)


def _bias_table(rel_bias, tq):
    nh = rel_bias.shape[0]
    tk = tq + LEFT_PAD
    rows = BIAS_ROWS_PER_STEP
    assert tq + rows - 1 <= 2 * tq and tk % rows == 0
    return pl.pallas_call(
        _bias_body,
        grid=(tk // rows,),
        in_specs=[pl.BlockSpec((nh, N_REL), lambda i: (0, 0))],
        out_specs=pl.BlockSpec((nh, rows, tq), lambda i: (0, i, 0)),
        out_shape=jax.ShapeDtypeStruct((nh, tk, tq), F32),
        compiler_params=_params(1),
        name="bias_table",
    )(rel_bias)


def _attn_step(mask_padding, x_ref, mod_ref, wq_ref, wo_ref, k_ref, vt_ref, bias_ref, lng_ref, lnb_ref,
               o_ref, ctxt_ref):
    tq, d = x_ref.shape
    n_sub = tq // KV_BLOCK
    win_blocks = (KV_BLOCK + LEFT_PAD) // KV_BLOCK
    tk = win_blocks * KV_BLOCK
    sub_k = ATTN_SUB_Q + LEFT_PAD
    hd = d // N_HEADS
    first_block = pl.program_id(1) * n_sub
    x = x_ref[...]
    h = _modulate(x, mod_ref, 1)
    q = jnp.dot(h.astype(BF16), wq_ref[...], preferred_element_type=F32) * (hd ** -0.5 * LOG2E)
    q = q.astype(BF16)

    first_head = lax.broadcasted_iota(jnp.int32, (KV_BLOCK, V7X_LANES), 1) < hd
    ones_rows = jnp.ones((V7X_BF16_SUBLANES, KV_BLOCK), BF16)

    def scores_t(sub, head):
        lanes = slice(head // 2 * V7X_LANES, (head // 2 + 1) * V7X_LANES)
        q_pair = q[sub * KV_BLOCK:(sub + 1) * KV_BLOCK, lanes]
        k_win = k_ref[pl.ds(first_block + sub, win_blocks), :, lanes].reshape(tk, V7X_LANES)
        q_one = jnp.where(first_head if head % 2 == 0 else ~first_head, q_pair, jnp.zeros_like(q_pair))
        return lax.dot_general(k_win, q_one, (((1,), (1,)), ((), ())), preferred_element_type=F32)

    def softmax_t(sub, head, s_t):
        first_key_row = LEFT_PAD - (first_block + sub) * KV_BLOCK
        probs = []
        for r0 in range(0, KV_BLOCK, ATTN_SUB_Q):
            s_sub = s_t[r0:r0 + sub_k, r0:r0 + ATTN_SUB_Q] + bias_ref[head]
            if mask_padding:
                row = lax.broadcasted_iota(jnp.int32, (sub_k, 1), 0)
                s_sub = jnp.where(row >= first_key_row - r0, s_sub, NEG_BIG)
            s_max = jnp.max(s_sub, axis=0, keepdims=True)
            pieces = [jnp.exp2(s_sub - s_max).astype(BF16)]
            if r0:
                pieces.insert(0, jnp.zeros((r0, ATTN_SUB_Q), BF16))
            if tk - sub_k - r0:
                pieces.append(jnp.zeros((tk - sub_k - r0, ATTN_SUB_Q), BF16))
            probs.append(jnp.concatenate(pieces, axis=0))
        return jnp.concatenate(probs, axis=1)

    def context_t(sub, head, p_t):
        rows = slice(head * hd, (head + 1) * hd)
        acc = None
        for n in range(win_blocks):
            v_t = vt_ref[pl.ds(first_block + sub + n, 1), rows, :].reshape(hd, KV_BLOCK)
            lhs = jnp.concatenate([v_t, ones_rows], axis=0)
            part = jnp.dot(lhs, p_t[n * KV_BLOCK:(n + 1) * KV_BLOCK], preferred_element_type=F32)
            acc = part if acc is None else acc + part
        ctxt_ref[rows, sub * KV_BLOCK:(sub + 1) * KV_BLOCK] = (acc[:hd] / acc[hd:hd + 1]).astype(BF16)

    items = [(sub, head) for sub in range(n_sub) for head in range(N_HEADS)]
    pending = [scores_t(*item) for item in items[:ATTN_HEADS_AHEAD]]
    for n, item in enumerate(items):
        s_cur = pending.pop(0)
        if n + ATTN_HEADS_AHEAD < len(items):
            pending.append(scores_t(*items[n + ATTN_HEADS_AHEAD]))
        context_t(*item, softmax_t(*item, s_cur))

    y = lax.dot_general(ctxt_ref[...], wo_ref[...], (((0,), (0,)), ((), ())), preferred_element_type=F32)
    r = ALPHA * x + _gate(mod_ref, 1) * y
    o_ref[...] = _layer_norm(r, lng_ref[1:2, :], lnb_ref[1:2, :])


def _attn_body(x_ref, *refs):
    has_padding = pl.program_id(1) * x_ref.shape[0] < LEFT_PAD
    pl.when(has_padding)(functools.partial(_attn_step, True, x_ref, *refs))
    pl.when(jnp.logical_not(has_padding))(functools.partial(_attn_step, False, x_ref, *refs))


def _attention(x, mod, w_q, w_o, k_pad, vt_pad, bias, ln_g, ln_b, *, layer, j):
    bsz, seq, d = x.shape
    tq = ATTN_TQ
    kv_blocks = k_pad.shape[1]
    return pl.pallas_call(
        _attn_body,
        grid=(bsz, seq // tq),
        in_specs=[
            pl.BlockSpec((None, tq, d), lambda b, i: (b, i, 0)),
            pl.BlockSpec((None, None, N_MOD, d), lambda b, i: (layer, b, 0, 0)),
            _resident((None, d, d), lambda b, i: (j, 0, 0)),
            _resident((None, d, d), lambda b, i: (j, 0, 0)),
            pl.BlockSpec((None, kv_blocks, KV_BLOCK, d), lambda b, i: (b, 0, 0, 0)),
            pl.BlockSpec((None, kv_blocks, d, KV_BLOCK), lambda b, i: (b, 0, 0, 0)),
            _resident((N_HEADS, ATTN_SUB_Q + LEFT_PAD, ATTN_SUB_Q), lambda b, i: (j, 0, 0)),
            _resident((None, 3, d), lambda b, i: (layer, 0, 0)),
            _resident((None, 3, d), lambda b, i: (layer, 0, 0)),
        ],
        out_specs=pl.BlockSpec((None, tq, d), lambda b, i: (b, i, 0)),
        out_shape=jax.ShapeDtypeStruct((bsz, seq, d), F32),
        scratch_shapes=[pltpu.VMEM((d, tq), BF16)],
        compiler_params=_params(2),
        name="attention",
    )(x, mod, w_q, w_o, k_pad, vt_pad, bias, ln_g, ln_b)


def kernel(x, c, w_ada, b_ada, ln_g, ln_b, ffn_gu, ffn_down, gmlp_w_in, gmlp_b_in, gmlp_ln_g, gmlp_ln_b,
           gmlp_w_s, gmlp_b_s, gmlp_w_out, w_ada_kv, b_ada_kv, w_kv, attn_w_q, attn_rel_bias, attn_w_o):
    bsz, seq, d = x.shape
    mod = _adaln(c, w_ada, b_ada, ADALN_TN).reshape(DEPTH, bsz, N_MOD, d)
    mkv = _adaln(c, w_ada_kv[None], b_ada_kv[None], ADALN_TN).reshape(bsz, 2, d)

    w_in_b = gmlp_w_in.astype(BF16)
    w_out_b = gmlp_w_out.astype(BF16)
    w_k_b = w_kv[:, :d].astype(BF16)
    w_v_t_b = w_kv[:, d:].T.astype(BF16)
    w_q_b = attn_w_q.astype(BF16)
    w_o_b = attn_w_o.astype(BF16)
    b_in = gmlp_b_in[:, None, :]
    gln_g = gmlp_ln_g[:, None, :]
    gln_b = gmlp_ln_b[:, None, :]
    b_s_t = jnp.swapaxes(gmlp_b_s, 1, 2)
    bias = _bias_table(attn_rel_bias.reshape(N_B * N_HEADS, N_REL), ATTN_SUB_Q)

    xf = x.reshape(bsz * seq, d)
    k_pad = vt_pad = None
    for l in range(DEPTH):
        ffn = functools.partial(_ffn, mod=mod, w_gu=ffn_gu, w_down=ffn_down, ln_g=ln_g, ln_b=ln_b,
                                layer=l, seq=seq, tm=FFN_TM)
        xf = ffn(xf, sub=0, half=0)
        if l < N_A:
            xf = _gmlp(xf, mod, w_in_b, b_in, gln_g, gln_b, gmlp_w_s, b_s_t, w_out_b, ln_g, ln_b,
                       layer=l, seq=seq, tm=GMLP_TM)
        else:
            xf = _attention(xf.reshape(bsz, seq, d), mod, w_q_b, w_o_b, k_pad, vt_pad, bias, ln_g, ln_b,
                            layer=l, j=l - N_A).reshape(bsz * seq, d)
        xf = ffn(xf, sub=2, half=1)
        if l == N_A - 1:
            k_pad, vt_pad = _kv_project(xf.reshape(bsz, seq, d), mkv, w_k_b, w_v_t_b)
    return xf.reshape(bsz, seq, d)
```

```python
import functools

import jax
import jax.numpy as jnp
from jax import lax
from jax.experimental import pallas as pl
from jax.experimental.pallas import tpu as pltpu

DEPTH = 4
CHUNK = 64
N_A = DEPTH // 2
N_B = DEPTH - N_A
GMLP_WINDOW = 128
GMLP_GROUPS = 8
N_HEADS = 16
LEFT_CHUNKS = 8
LEFT_PAD = LEFT_CHUNKS * CHUNK
MAX_REL = 4 * CHUNK
N_REL = (CHUNK - 1) + MAX_REL + 1
ALPHA = (2.0 * DEPTH) ** 0.25
LN_EPS = 1e-5
N_MOD = 9

V7X_LANES = 128
V7X_MXU_N = 256
V7X_BF16_SUBLANES = 16
V7X_VMEM_LIMIT_BYTES = 58 * 1024 * 1024

NEG_BIG = -1e30
LOG2E = 1.4426950408889634

KV_BLOCK = 256
ATTN_TQ = 512
ATTN_SUB_Q = 128
ATTN_HEADS_AHEAD = 3

FFN_TM = 1024
FFN_FINISH_BLOCKS = 8
FFN_FIRST_FINISH_CHUNK = 1
FFN_WEIGHT_SLOTS = 2
GMLP_TM = 512
GMLP_DOTS_AHEAD = 3
ADALN_TN = 3072
BIAS_ROWS_PER_STEP = 64

F32 = jnp.float32
BF16 = jnp.bfloat16


def _params(n_axes):
    return pltpu.CompilerParams(
        dimension_semantics=("arbitrary",) * n_axes,
        vmem_limit_bytes=V7X_VMEM_LIMIT_BYTES,
    )


def _resident(block_shape, index_map):
    return pl.BlockSpec(block_shape, index_map, pipeline_mode=pl.Buffered(1))


def _layer_norm(r, g, b):
    mu = jnp.mean(r, axis=-1, keepdims=True)
    d = r - mu
    var = jnp.mean(d * d, axis=-1, keepdims=True)
    return d * lax.rsqrt(var + LN_EPS) * g + b


def _gelu(z):
    return 0.5 * z * (1.0 + lax.erf(z * (0.5 ** 0.5)))


def _modulate(x, mod_ref, sub):
    shift = mod_ref[3 * sub:3 * sub + 1, :]
    scale = mod_ref[3 * sub + 1:3 * sub + 2, :]
    return x * (1.0 + scale) + shift


def _gate(mod_ref, sub):
    return 1.0 + mod_ref[3 * sub + 2:3 * sub + 3, :]


def _zero_bits_of(v):
    bits = lax.bitcast_convert_type(v, jnp.int32)
    sixteen = jnp.full(bits.shape, 16, jnp.int32)
    cleared = lax.shift_right_logical(lax.shift_right_logical(bits, sixteen), sixteen)
    return jnp.max(cleared, axis=0, keepdims=True)


def _after(v, finished_rows):
    zero = _zero_bits_of(finished_rows)[:, :v.shape[1]]
    return jnp.where(zero == 0, v, jnp.zeros_like(v))


def _finish_rows(sub, n_blocks, lng_ref, lnb_ref, r_ref, o_ref, k):
    n = o_ref.shape[0] // n_blocks
    rows = slice(k * n, (k + 1) * n)
    out = _layer_norm(r_ref[rows, :], lng_ref[sub:sub + 1, :], lnb_ref[sub:sub + 1, :])
    o_ref[rows, :] = out
    return out


def _pipelined_specs(n_tiles, tiles_per_seq, tm, d, layer):
    cur = lambda i: jnp.minimum(i, n_tiles - 1)
    prev = lambda i: jnp.maximum(i - 1, 0)
    in_specs = [
        pl.BlockSpec((tm, d), lambda i: (cur(i), 0)),
        pl.BlockSpec((None, None, N_MOD, d), lambda i: (layer, cur(i) // tiles_per_seq, 0, 0)),
    ]
    out_spec = pl.BlockSpec((tm, d), lambda i: (prev(i), 0))
    parking = [pltpu.VMEM((tm, d), F32)]
    return in_specs, out_spec, parking


def _adaln_body(c_ref, w_ref, b_ref, o_ref):
    c = c_ref[...]
    c_act = c * jax.nn.sigmoid(c)
    o_ref[...] = jnp.dot(c_act, w_ref[...], preferred_element_type=F32) + b_ref[...]


def _adaln(c, w, b, tn):
    n_l, d, n = w.shape
    bsz = c.shape[0]
    return pl.pallas_call(
        _adaln_body,
        grid=(n_l, n // tn),
        in_specs=[
            pl.BlockSpec((bsz, d), lambda l, j: (0, 0)),
            pl.BlockSpec((None, d, tn), lambda l, j: (l, 0, j)),
            pl.BlockSpec((None, 1, tn), lambda l, j: (l, 0, j)),
        ],
        out_specs=pl.BlockSpec((None, bsz, tn), lambda l, j: (l, 0, j)),
        out_shape=jax.ShapeDtypeStruct((n_l, bsz, n), F32),
        compiler_params=_params(2),
        name="adaln",
    )(c, w, b.reshape(n_l, 1, n))


def _ffn_weight_copies(layer, half, d_ff, wgu_hbm, wd_hbm, stage_gu_ref, stage_d_ref, sem_ref, j):
    lo = j * V7X_MXU_N
    slot = j % FFN_WEIGHT_SLOTS
    return (
        pltpu.make_async_copy(wgu_hbm.at[layer, half, :, pl.ds(lo, V7X_MXU_N)],
                              stage_gu_ref.at[slot, 0], sem_ref.at[slot, 0]),
        pltpu.make_async_copy(wgu_hbm.at[layer, half, :, pl.ds(d_ff + lo, V7X_MXU_N)],
                              stage_gu_ref.at[slot, 1], sem_ref.at[slot, 1]),
        pltpu.make_async_copy(wd_hbm.at[layer, half, pl.ds(lo, V7X_MXU_N), :],
                              stage_d_ref.at[slot], sem_ref.at[slot, 2]),
    )


def _ffn_take_chunk(copies, n_chunks, d_ff, wgu_ref, wd_ref, stage_gu_ref, stage_d_ref, j):
    for copy in copies(j):
        copy.wait()
    lo = j * V7X_MXU_N
    slot = j % FFN_WEIGHT_SLOTS
    wgu_ref[:, lo:lo + V7X_MXU_N] = stage_gu_ref[slot, 0].astype(BF16)
    wgu_ref[:, d_ff + lo:d_ff + lo + V7X_MXU_N] = stage_gu_ref[slot, 1].astype(BF16)
    wd_ref[lo:lo + V7X_MXU_N, :] = stage_d_ref[slot].astype(BF16)
    if j + FFN_WEIGHT_SLOTS < n_chunks:
        for copy in copies(j + FFN_WEIGHT_SLOTS):
            copy.start()


def _ffn_start(sub, d_ff, x_ref, mod_ref, wgu_ref, wd_ref, a_ref, r_ref, finish_rows=None, take_chunk=None):
    x = x_ref[...]
    h = _modulate(x, mod_ref, sub).astype(BF16)
    for j in range(d_ff // V7X_MXU_N):
        lo = j * V7X_MXU_N
        if take_chunk is not None:
            take_chunk(j)
        g = jnp.dot(h, wgu_ref[:, lo:lo + V7X_MXU_N], preferred_element_type=F32)
        u = jnp.dot(h, wgu_ref[:, d_ff + lo:d_ff + lo + V7X_MXU_N], preferred_element_type=F32)
        act = g * jax.nn.sigmoid(g) * u
        k = j - FFN_FIRST_FINISH_CHUNK
        if finish_rows is not None and 0 <= k < FFN_FINISH_BLOCKS:
            act = _after(act, finish_rows(k))
        a_ref[:, lo:lo + V7X_MXU_N] = act.astype(BF16)
    y = jnp.dot(a_ref[...], wd_ref[...], preferred_element_type=F32)
    r_ref[...] = ALPHA * x + (0.5 * _gate(mod_ref, sub)) * y


def _ffn_body(layer, half, sub, d_ff, x_ref, mod_ref, wgu_hbm, wd_hbm, lng_ref, lnb_ref, o_ref,
              wgu_ref, wd_ref, stage_gu_ref, stage_d_ref, sem_ref, a_ref, r_ref):
    i = pl.program_id(0)
    last = pl.num_programs(0) - 1
    n_chunks = d_ff // V7X_MXU_N
    start = functools.partial(_ffn_start, sub, d_ff, x_ref, mod_ref, wgu_ref, wd_ref, a_ref, r_ref)
    finish_rows = functools.partial(_finish_rows, sub, FFN_FINISH_BLOCKS, lng_ref, lnb_ref, r_ref, o_ref)
    copies = functools.partial(_ffn_weight_copies, layer, half, d_ff, wgu_hbm, wd_hbm, stage_gu_ref, stage_d_ref,
                               sem_ref)
    take_chunk = functools.partial(_ffn_take_chunk, copies, n_chunks, d_ff, wgu_ref, wd_ref, stage_gu_ref,
                                   stage_d_ref)

    @pl.when(i == 0)
    def _():
        for j in range(FFN_WEIGHT_SLOTS):
            for copy in copies(j):
                copy.start()
        start(take_chunk=take_chunk)

    @pl.when((i > 0) & (i < last))
    def _():
        start(finish_rows=finish_rows)

    @pl.when(i == last)
    def _():
        for k in range(FFN_FINISH_BLOCKS):
            finish_rows(k)


def _ffn(x, mod, w_gu, w_down, ln_g, ln_b, *, layer, sub, half, seq, tm):
    m, d = x.shape
    d_ff = w_down.shape[2]
    n_tiles = m // tm
    in_specs, out_spec, parking = _pipelined_specs(n_tiles, seq // tm, tm, d, layer)
    return pl.pallas_call(
        functools.partial(_ffn_body, layer, half, sub, d_ff),
        grid=(n_tiles + 1,),
        in_specs=in_specs + [
            pl.BlockSpec(memory_space=pl.ANY),
            pl.BlockSpec(memory_space=pl.ANY),
            _resident((None, 3, d), lambda i: (layer, 0, 0)),
            _resident((None, 3, d), lambda i: (layer, 0, 0)),
        ],
        out_specs=out_spec,
        out_shape=jax.ShapeDtypeStruct((m, d), F32),
        scratch_shapes=[
            pltpu.VMEM((d, 2 * d_ff), BF16),
            pltpu.VMEM((d_ff, d), BF16),
            pltpu.VMEM((FFN_WEIGHT_SLOTS, 2, d, V7X_MXU_N), F32),
            pltpu.VMEM((FFN_WEIGHT_SLOTS, V7X_MXU_N, d), F32),
            pltpu.SemaphoreType.DMA((FFN_WEIGHT_SLOTS, 3)),
            pltpu.VMEM((tm, d_ff), BF16),
        ] + parking,
        compiler_params=_params(1),
        name="ffn",
    )(x, mod, w_gu, w_down, ln_g, ln_b)


def _gmlp_body(x_ref, mod_ref, win_ref, bin_ref, glng_ref, glnb_ref, ws_ref, bst_ref, wout_ref,
               lng_ref, lnb_ref, o_ref, v_ref, m_ref):
    tm = x_ref.shape[0]
    half = wout_ref.shape[0]
    gdim = half // GMLP_GROUPS
    x = x_ref[...]
    h = _modulate(x, mod_ref, 1).astype(BF16)

    def project(lo):
        return jnp.dot(h, win_ref[:, lo:lo + gdim], preferred_element_type=F32) + bin_ref[:, lo:lo + gdim]

    order = [half + g * gdim for g in range(GMLP_GROUPS)] + [g * gdim for g in range(GMLP_GROUPS)]
    pending = [project(lo) for lo in order[:GMLP_DOTS_AHEAD]]

    def next_projection(n):
        z = pending.pop(0)
        if n + GMLP_DOTS_AHEAD < len(order):
            pending.append(project(order[n + GMLP_DOTS_AHEAD]))
        return z

    for g in range(GMLP_GROUPS):
        v_ref[:, g * gdim:(g + 1) * gdim] = _gelu(next_projection(g))
    v = v_ref[...]
    mu = jnp.mean(v, axis=-1, keepdims=True)
    dv = v - mu
    rstd = lax.rsqrt(jnp.mean(dv * dv, axis=-1, keepdims=True) + LN_EPS)

    t_out = lax.broadcasted_iota(jnp.int32, (GMLP_WINDOW, GMLP_WINDOW), 0)
    s_in = lax.broadcasted_iota(jnp.int32, (GMLP_WINDOW, GMLP_WINDOW), 1)
    causal = (s_in // CHUNK) <= (t_out // CHUNK)

    for g in range(GMLP_GROUPS):
        lo = g * gdim
        vn = ((v_ref[:, lo:lo + gdim] - mu) * rstd * glng_ref[:, lo:lo + gdim]
              + glnb_ref[:, lo:lo + gdim]).astype(BF16)
        ws_g = jnp.where(causal, ws_ref[g], 0.0).astype(BF16)
        b_col = bst_ref[:, g:g + 1]
        u = _gelu(next_projection(GMLP_GROUPS + g))
        for w in range(tm // GMLP_WINDOW):
            rows = slice(w * GMLP_WINDOW, (w + 1) * GMLP_WINDOW)
            s = jnp.dot(ws_g, vn[rows], preferred_element_type=F32) + b_col
            m_ref[rows, lo:lo + gdim] = (u[rows] * s).astype(BF16)

    y = jnp.dot(m_ref[...], wout_ref[...], preferred_element_type=F32)
    r = ALPHA * x + _gate(mod_ref, 1) * y
    o_ref[...] = _layer_norm(r, lng_ref[1:2, :], lnb_ref[1:2, :])


def _gmlp(x, mod, w_in, b_in, gln_g, gln_b, w_s, b_s_t, w_out, ln_g, ln_b, *, layer, seq, tm):
    m, d = x.shape
    width = w_in.shape[2]
    half = width // 2
    tiles_per_seq = seq // tm
    const = lambda i: (layer, 0, 0)
    return pl.pallas_call(
        _gmlp_body,
        grid=(m // tm,),
        in_specs=[
            pl.BlockSpec((tm, d), lambda i: (i, 0)),
            pl.BlockSpec((None, None, N_MOD, d), lambda i: (layer, i // tiles_per_seq, 0, 0)),
            _resident((None, d, width), const),
            _resident((None, 1, width), const),
            _resident((None, 1, half), const),
            _resident((None, 1, half), const),
            _resident((None, GMLP_GROUPS, GMLP_WINDOW, GMLP_WINDOW), lambda i: (layer, 0, 0, 0)),
            _resident((None, GMLP_WINDOW, GMLP_GROUPS), const),
            _resident((None, half, d), const),
            _resident((None, 3, d), const),
            _resident((None, 3, d), const),
        ],
        out_specs=pl.BlockSpec((tm, d), lambda i: (i, 0)),
        out_shape=jax.ShapeDtypeStruct((m, d), F32),
        scratch_shapes=[pltpu.VMEM((tm, half), F32), pltpu.VMEM((tm, half), BF16)],
        compiler_params=_params(1),
        name="gmlp",
    )(x, mod, w_in, b_in, gln_g, gln_b, w_s, b_s_t, w_out, ln_g, ln_b)


def _kv_body(x_ref, mkv_ref, wk_ref, wvt_ref, k_ref, vt_ref):
    i = pl.program_id(1)

    @pl.when(i == 0)
    def _():
        k_ref[...] = jnp.zeros_like(k_ref)
        vt_ref[...] = jnp.zeros_like(vt_ref)

    @pl.when(i > 0)
    def _():
        h = (x_ref[...] * (1.0 + mkv_ref[1:2, :]) + mkv_ref[0:1, :]).astype(BF16)
        k = jnp.dot(h, wk_ref[...], preferred_element_type=F32)
        k_ref[...] = k.astype(BF16).reshape(k_ref.shape)
        vt = lax.dot_general(wvt_ref[...], h, (((1,), (1,)), ((), ())), preferred_element_type=F32)
        for n in range(vt_ref.shape[0]):
            vt_ref[n] = vt[:, n * KV_BLOCK:(n + 1) * KV_BLOCK].astype(BF16)


def _kv_project(x, mkv, w_k, w_v_t):
    bsz, seq, d = x.shape
    tm = LEFT_PAD
    blocks_per_tile = tm // KV_BLOCK
    n_blocks = (LEFT_PAD + seq) // KV_BLOCK
    return pl.pallas_call(
        _kv_body,
        grid=(bsz, 1 + seq // tm),
        in_specs=[
            pl.BlockSpec((None, tm, d), lambda b, i: (b, jnp.maximum(i - 1, 0), 0)),
            pl.BlockSpec((None, 2, d), lambda b, i: (b, 0, 0)),
            _resident((d, d), lambda b, i: (0, 0)),
            _resident((d, d), lambda b, i: (0, 0)),
        ],
        out_specs=[
            pl.BlockSpec((None, blocks_per_tile, KV_BLOCK, d), lambda b, i: (b, i, 0, 0)),
            pl.BlockSpec((None, blocks_per_tile, d, KV_BLOCK), lambda b, i: (b, i, 0, 0)),
        ],
        out_shape=[
            jax.ShapeDtypeStruct((bsz, n_blocks, KV_BLOCK, d), BF16),
            jax.ShapeDtypeStruct((bsz, n_blocks, d, KV_BLOCK), BF16),
        ],
        compiler_params=_params(2),
        name="kv_project",
    )(x, mkv, w_k, w_v_t)


def _bias_body(rb_ref, o_ref):
    rows, tq = o_ref.shape[1], o_ref.shape[2]
    width = 2 * tq
    r0 = pl.program_id(0) * rows
    rb = rb_ref[...]
    hi = rb.astype(BF16)
    rem = rb - hi.astype(F32)
    mid = rem.astype(BF16)
    lo = (rem - mid.astype(F32)).astype(BF16)
    lane = lax.broadcasted_iota(jnp.int32, (1, width), 1)
    rel = lax.broadcasted_iota(jnp.int32, (N_REL, width), 0)
    idx = jnp.clip(lane - (rows - 1) - r0 + LEFT_PAD, -(CHUNK - 1), MAX_REL) + (CHUNK - 1)
    onehot = (rel == idx).astype(BF16)
    wide = (jnp.dot(hi, onehot, preferred_element_type=F32)
            + jnp.dot(mid, onehot, preferred_element_type=F32)
            + jnp.dot(lo, onehot, preferred_element_type=F32)) * LOG2E
    qc = lax.broadcasted_iota(jnp.int32, (1, tq), 1) // CHUNK
    for k in range(rows):
        kc = (r0 + k) // CHUNK
        visible = (kc >= qc) & (kc <= qc + LEFT_CHUNKS)
        o_ref[:, k, :] = jnp.where(visible, wide[:, rows - 1 - k:rows - 1 - k + tq], NEG_BIG)


def _bias_table(rel_bias, tq):
    nh = rel_bias.shape[0]
    tk = tq + LEFT_PAD
    rows = BIAS_ROWS_PER_STEP
    assert tq + rows - 1 <= 2 * tq and tk % rows == 0
    return pl.pallas_call(
        _bias_body,
        grid=(tk // rows,),
        in_specs=[pl.BlockSpec((nh, N_REL), lambda i: (0, 0))],
        out_specs=pl.BlockSpec((nh, rows, tq), lambda i: (0, i, 0)),
        out_shape=jax.ShapeDtypeStruct((nh, tk, tq), F32),
        compiler_params=_params(1),
        name="bias_table",
    )(rel_bias)


def _attn_step(mask_padding, x_ref, mod_ref, wq_ref, wo_ref, k_ref, vt_ref, bias_ref, lng_ref, lnb_ref,
               o_ref, ctxt_ref):
    tq, d = x_ref.shape
    n_sub = tq // KV_BLOCK
    win_blocks = (KV_BLOCK + LEFT_PAD) // KV_BLOCK
    tk = win_blocks * KV_BLOCK
    sub_k = ATTN_SUB_Q + LEFT_PAD
    hd = d // N_HEADS
    first_block = pl.program_id(1) * n_sub
    x = x_ref[...]
    h = _modulate(x, mod_ref, 1)
    q = jnp.dot(h.astype(BF16), wq_ref[...], preferred_element_type=F32) * (hd ** -0.5 * LOG2E)
    q = q.astype(BF16)

    first_head = lax.broadcasted_iota(jnp.int32, (KV_BLOCK, V7X_LANES), 1) < hd
    ones_rows = jnp.ones((V7X_BF16_SUBLANES, KV_BLOCK), BF16)

    def scores_t(sub, head):
        lanes = slice(head // 2 * V7X_LANES, (head // 2 + 1) * V7X_LANES)
        q_pair = q[sub * KV_BLOCK:(sub + 1) * KV_BLOCK, lanes]
        k_win = k_ref[pl.ds(first_block + sub, win_blocks), :, lanes].reshape(tk, V7X_LANES)
        q_one = jnp.where(first_head if head % 2 == 0 else ~first_head, q_pair, jnp.zeros_like(q_pair))
        return lax.dot_general(k_win, q_one, (((1,), (1,)), ((), ())), preferred_element_type=F32)

    def softmax_t(sub, head, s_t):
        first_key_row = LEFT_PAD - (first_block + sub) * KV_BLOCK
        probs = []
        for r0 in range(0, KV_BLOCK, ATTN_SUB_Q):
            s_sub = s_t[r0:r0 + sub_k, r0:r0 + ATTN_SUB_Q] + bias_ref[head]
            if mask_padding:
                row = lax.broadcasted_iota(jnp.int32, (sub_k, 1), 0)
                s_sub = jnp.where(row >= first_key_row - r0, s_sub, NEG_BIG)
            s_max = jnp.max(s_sub, axis=0, keepdims=True)
            pieces = [jnp.exp2(s_sub - s_max).astype(BF16)]
            if r0:
                pieces.insert(0, jnp.zeros((r0, ATTN_SUB_Q), BF16))
            if tk - sub_k - r0:
                pieces.append(jnp.zeros((tk - sub_k - r0, ATTN_SUB_Q), BF16))
            probs.append(jnp.concatenate(pieces, axis=0))
        return jnp.concatenate(probs, axis=1)

    def context_t(sub, head, p_t):
        rows = slice(head * hd, (head + 1) * hd)
        acc = None
        for n in range(win_blocks):
            v_t = vt_ref[pl.ds(first_block + sub + n, 1), rows, :].reshape(hd, KV_BLOCK)
            lhs = jnp.concatenate([v_t, ones_rows], axis=0)
            part = jnp.dot(lhs, p_t[n * KV_BLOCK:(n + 1) * KV_BLOCK], preferred_element_type=F32)
            acc = part if acc is None else acc + part
        ctxt_ref[rows, sub * KV_BLOCK:(sub + 1) * KV_BLOCK] = (acc[:hd] / acc[hd:hd + 1]).astype(BF16)

    items = [(sub, head) for sub in range(n_sub) for head in range(N_HEADS)]
    pending = [scores_t(*item) for item in items[:ATTN_HEADS_AHEAD]]
    for n, item in enumerate(items):
        s_cur = pending.pop(0)
        if n + ATTN_HEADS_AHEAD < len(items):
            pending.append(scores_t(*items[n + ATTN_HEADS_AHEAD]))
        context_t(*item, softmax_t(*item, s_cur))

    y = lax.dot_general(ctxt_ref[...], wo_ref[...], (((0,), (0,)), ((), ())), preferred_element_type=F32)
    r = ALPHA * x + _gate(mod_ref, 1) * y
    o_ref[...] = _layer_norm(r, lng_ref[1:2, :], lnb_ref[1:2, :])


def _attn_body(x_ref, *refs):
    has_padding = pl.program_id(1) * x_ref.shape[0] < LEFT_PAD
    pl.when(has_padding)(functools.partial(_attn_step, True, x_ref, *refs))
    pl.when(jnp.logical_not(has_padding))(functools.partial(_attn_step, False, x_ref, *refs))


def _attention(x, mod, w_q, w_o, k_pad, vt_pad, bias, ln_g, ln_b, *, layer, j):
    bsz, seq, d = x.shape
    tq = ATTN_TQ
    kv_blocks = k_pad.shape[1]
    return pl.pallas_call(
        _attn_body,
        grid=(bsz, seq // tq),
        in_specs=[
            pl.BlockSpec((None, tq, d), lambda b, i: (b, i, 0)),
            pl.BlockSpec((None, None, N_MOD, d), lambda b, i: (layer, b, 0, 0)),
            _resident((None, d, d), lambda b, i: (j, 0, 0)),
            _resident((None, d, d), lambda b, i: (j, 0, 0)),
            pl.BlockSpec((None, kv_blocks, KV_BLOCK, d), lambda b, i: (b, 0, 0, 0)),
            pl.BlockSpec((None, kv_blocks, d, KV_BLOCK), lambda b, i: (b, 0, 0, 0)),
            _resident((N_HEADS, ATTN_SUB_Q + LEFT_PAD, ATTN_SUB_Q), lambda b, i: (j, 0, 0)),
            _resident((None, 3, d), lambda b, i: (layer, 0, 0)),
            _resident((None, 3, d), lambda b, i: (layer, 0, 0)),
        ],
        out_specs=pl.BlockSpec((None, tq, d), lambda b, i: (b, i, 0)),
        out_shape=jax.ShapeDtypeStruct((bsz, seq, d), F32),
        scratch_shapes=[pltpu.VMEM((d, tq), BF16)],
        compiler_params=_params(2),
        name="attention",
    )(x, mod, w_q, w_o, k_pad, vt_pad, bias, ln_g, ln_b)


def kernel(x, c, w_ada, b_ada, ln_g, ln_b, ffn_gu, ffn_down, gmlp_w_in, gmlp_b_in, gmlp_ln_g, gmlp_ln_b,
           gmlp_w_s, gmlp_b_s, gmlp_w_out, w_ada_kv, b_ada_kv, w_kv, attn_w_q, attn_rel_bias, attn_w_o):
    bsz, seq, d = x.shape
    mod = _adaln(c, w_ada, b_ada, ADALN_TN).reshape(DEPTH, bsz, N_MOD, d)
    mkv = _adaln(c, w_ada_kv[None], b_ada_kv[None], w_ada_kv.shape[1]).reshape(bsz, 2, d)

    w_in_b = gmlp_w_in.astype(BF16)
    w_out_b = gmlp_w_out.astype(BF16)
    w_k_b = w_kv[:, :d].astype(BF16)
    w_v_t_b = w_kv[:, d:].T.astype(BF16)
    w_q_b = attn_w_q.astype(BF16)
    w_o_b = attn_w_o.astype(BF16)
    b_in = gmlp_b_in[:, None, :]
    gln_g = gmlp_ln_g[:, None, :]
    gln_b = gmlp_ln_b[:, None, :]
    b_s_t = jnp.swapaxes(gmlp_b_s, 1, 2)
    bias = _bias_table(attn_rel_bias.reshape(N_B * N_HEADS, N_REL), ATTN_SUB_Q)

    xf = x.reshape(bsz * seq, d)
    k_pad = vt_pad = None
    for l in range(DEPTH):
        ffn = functools.partial(_ffn, mod=mod, w_gu=ffn_gu, w_down=ffn_down, ln_g=ln_g, ln_b=ln_b,
                                layer=l, seq=seq, tm=FFN_TM)
        xf = ffn(xf, sub=0, half=0)
        if l < N_A:
            xf = _gmlp(xf, mod, w_in_b, b_in, gln_g, gln_b, gmlp_w_s, b_s_t, w_out_b, ln_g, ln_b,
                       layer=l, seq=seq, tm=GMLP_TM)
        else:
            xf = _attention(xf.reshape(bsz, seq, d), mod, w_q_b, w_o_b, k_pad, vt_pad, bias, ln_g, ln_b,
                            layer=l, j=l - N_A).reshape(bsz * seq, d)
        xf = ffn(xf, sub=2, half=1)
        if l == N_A - 1:
            k_pad, vt_pad = _kv_project(xf.reshape(bsz, seq, d), mkv, w_k_b, w_v_t_b)
    return xf.reshape(bsz, seq, d)
```

```python
import functools

import jax
import jax.numpy as jnp
from jax import lax
from jax.experimental import pallas as pl
from jax.experimental.pallas import tpu as pltpu

DEPTH = 4
CHUNK = 64
N_A = DEPTH // 2
N_B = DEPTH - N_A
GMLP_WINDOW = 128
GMLP_GROUPS = 8
N_HEADS = 16
LEFT_CHUNKS = 8
LEFT_PAD = LEFT_CHUNKS * CHUNK
MAX_REL = 4 * CHUNK
N_REL = (CHUNK - 1) + MAX_REL + 1
ALPHA = (2.0 * DEPTH) ** 0.25
LN_EPS = 1e-5
N_MOD = 9

V7X_LANES = 128
V7X_MXU_N = 256
V7X_BF16_SUBLANES = 16
V7X_VMEM_LIMIT_BYTES = 56 * 1024 * 1024

NEG_BIG = -1e30
LOG2E = 1.4426950408889634

KV_BLOCK = 256
ATTN_TQ = 512
ATTN_SUB_Q = 128
ATTN_HEADS_AHEAD = 3

FFN_TM = 512
FFN_FINISH_BLOCKS = 8
FFN_FIRST_FINISH_CHUNK = 1
FFN_WEIGHT_SLOTS = 3
GMLP_TM = 512
GMLP_DOTS_AHEAD = 3
ADALN_TN = 3072
BIAS_ROWS_PER_STEP = 64

F32 = jnp.float32
BF16 = jnp.bfloat16


def _params(n_axes):
    return pltpu.CompilerParams(
        dimension_semantics=("arbitrary",) * n_axes,
        vmem_limit_bytes=V7X_VMEM_LIMIT_BYTES,
    )


def _resident(block_shape, index_map):
    return pl.BlockSpec(block_shape, index_map, pipeline_mode=pl.Buffered(1))


def _layer_norm(r, g, b):
    mu = jnp.mean(r, axis=-1, keepdims=True)
    d = r - mu
    var = jnp.mean(d * d, axis=-1, keepdims=True)
    return d * lax.rsqrt(var + LN_EPS) * g + b


def _gelu(z):
    return 0.5 * z * (1.0 + lax.erf(z * (0.5 ** 0.5)))


def _modulate(x, mod_ref, sub):
    shift = mod_ref[3 * sub:3 * sub + 1, :]
    scale = mod_ref[3 * sub + 1:3 * sub + 2, :]
    return x * (1.0 + scale) + shift


def _gate(mod_ref, sub):
    return 1.0 + mod_ref[3 * sub + 2:3 * sub + 3, :]


def _zero_bits_of(v):
    bits = lax.bitcast_convert_type(v, jnp.int32)
    sixteen = jnp.full(bits.shape, 16, jnp.int32)
    cleared = lax.shift_right_logical(lax.shift_right_logical(bits, sixteen), sixteen)
    return jnp.max(cleared, axis=0, keepdims=True)


def _after(v, finished_rows):
    zero = _zero_bits_of(finished_rows)[:, :v.shape[1]]
    return jnp.where(zero == 0, v, jnp.zeros_like(v))


def _finish_rows(sub, n_blocks, lng_ref, lnb_ref, r_ref, o_ref, k):
    n = o_ref.shape[0] // n_blocks
    rows = slice(k * n, (k + 1) * n)
    out = _layer_norm(r_ref[rows, :], lng_ref[sub:sub + 1, :], lnb_ref[sub:sub + 1, :])
    o_ref[rows, :] = out
    return out


def _pipelined_specs(n_tiles, tiles_per_seq, tm, d, layer):
    cur = lambda i: jnp.minimum(i, n_tiles - 1)
    prev = lambda i: jnp.maximum(i - 1, 0)
    in_specs = [
        pl.BlockSpec((tm, d), lambda i: (cur(i), 0)),
        pl.BlockSpec((None, None, N_MOD, d), lambda i: (layer, cur(i) // tiles_per_seq, 0, 0)),
    ]
    out_spec = pl.BlockSpec((tm, d), lambda i: (prev(i), 0))
    parking = [pltpu.VMEM((tm, d), F32)]
    return in_specs, out_spec, parking


def _adaln_body(c_ref, w_ref, b_ref, o_ref):
    c = c_ref[...]
    c_act = c * jax.nn.sigmoid(c)
    o_ref[...] = jnp.dot(c_act, w_ref[...], preferred_element_type=F32) + b_ref[...]


def _adaln(c, w, b, tn):
    n_l, d, n = w.shape
    bsz = c.shape[0]
    return pl.pallas_call(
        _adaln_body,
        grid=(n_l, n // tn),
        in_specs=[
            pl.BlockSpec((bsz, d), lambda l, j: (0, 0)),
            pl.BlockSpec((None, d, tn), lambda l, j: (l, 0, j)),
            pl.BlockSpec((None, 1, tn), lambda l, j: (l, 0, j)),
        ],
        out_specs=pl.BlockSpec((None, bsz, tn), lambda l, j: (l, 0, j)),
        out_shape=jax.ShapeDtypeStruct((n_l, bsz, n), F32),
        compiler_params=_params(2),
        name="adaln",
    )(c, w, b.reshape(n_l, 1, n))


def _ffn_weight_copies(layer, half, d_ff, wgu_hbm, wd_hbm, stage_gu_ref, stage_d_ref, sem_ref, j):
    lo = j * V7X_MXU_N
    slot = j % FFN_WEIGHT_SLOTS
    return (
        pltpu.make_async_copy(wgu_hbm.at[layer, half, :, pl.ds(lo, V7X_MXU_N)],
                              stage_gu_ref.at[slot, 0], sem_ref.at[slot, 0]),
        pltpu.make_async_copy(wgu_hbm.at[layer, half, :, pl.ds(d_ff + lo, V7X_MXU_N)],
                              stage_gu_ref.at[slot, 1], sem_ref.at[slot, 1]),
        pltpu.make_async_copy(wd_hbm.at[layer, half, pl.ds(lo, V7X_MXU_N), :],
                              stage_d_ref.at[slot], sem_ref.at[slot, 2]),
    )


def _ffn_take_chunk(copies, n_chunks, d_ff, wgu_ref, wd_ref, stage_gu_ref, stage_d_ref, j):
    for copy in copies(j):
        copy.wait()
    lo = j * V7X_MXU_N
    slot = j % FFN_WEIGHT_SLOTS
    wgu_ref[:, lo:lo + V7X_MXU_N] = stage_gu_ref[slot, 0].astype(BF16)
    wgu_ref[:, d_ff + lo:d_ff + lo + V7X_MXU_N] = stage_gu_ref[slot, 1].astype(BF16)
    wd_ref[lo:lo + V7X_MXU_N, :] = stage_d_ref[slot].astype(BF16)
    if j + FFN_WEIGHT_SLOTS < n_chunks:
        for copy in copies(j + FFN_WEIGHT_SLOTS):
            copy.start()


def _ffn_start(sub, d_ff, x_ref, mod_ref, wgu_ref, wd_ref, a_ref, r_ref, finish_rows=None, take_chunk=None):
    x = x_ref[...]
    h = _modulate(x, mod_ref, sub).astype(BF16)
    for j in range(d_ff // V7X_MXU_N):
        lo = j * V7X_MXU_N
        if take_chunk is not None:
            take_chunk(j)
        g = jnp.dot(h, wgu_ref[:, lo:lo + V7X_MXU_N], preferred_element_type=F32)
        u = jnp.dot(h, wgu_ref[:, d_ff + lo:d_ff + lo + V7X_MXU_N], preferred_element_type=F32)
        act = g * jax.nn.sigmoid(g) * u
        k = j - FFN_FIRST_FINISH_CHUNK
        if finish_rows is not None and 0 <= k < FFN_FINISH_BLOCKS:
            act = _after(act, finish_rows(k))
        a_ref[:, lo:lo + V7X_MXU_N] = act.astype(BF16)
    y = jnp.dot(a_ref[...], wd_ref[...], preferred_element_type=F32)
    r_ref[...] = ALPHA * x + (0.5 * _gate(mod_ref, sub)) * y


def _ffn_body(layer, half, sub, d_ff, x_ref, mod_ref, wgu_hbm, wd_hbm, lng_ref, lnb_ref, o_ref,
              wgu_ref, wd_ref, stage_gu_ref, stage_d_ref, sem_ref, a_ref, r_ref):
    i = pl.program_id(0)
    last = pl.num_programs(0) - 1
    n_chunks = d_ff // V7X_MXU_N
    start = functools.partial(_ffn_start, sub, d_ff, x_ref, mod_ref, wgu_ref, wd_ref, a_ref, r_ref)
    finish_rows = functools.partial(_finish_rows, sub, FFN_FINISH_BLOCKS, lng_ref, lnb_ref, r_ref, o_ref)
    copies = functools.partial(_ffn_weight_copies, layer, half, d_ff, wgu_hbm, wd_hbm, stage_gu_ref, stage_d_ref,
                               sem_ref)
    take_chunk = functools.partial(_ffn_take_chunk, copies, n_chunks, d_ff, wgu_ref, wd_ref, stage_gu_ref,
                                   stage_d_ref)

    @pl.when(i == 0)
    def _():
        for j in range(FFN_WEIGHT_SLOTS):
            for copy in copies(j):
                copy.start()
        start(take_chunk=take_chunk)

    @pl.when((i > 0) & (i < last))
    def _():
        start(finish_rows=finish_rows)

    @pl.when(i == last)
    def _():
        for k in range(FFN_FINISH_BLOCKS):
            finish_rows(k)


def _ffn(x, mod, w_gu, w_down, ln_g, ln_b, *, layer, sub, half, seq, tm):
    m, d = x.shape
    d_ff = w_down.shape[2]
    n_tiles = m // tm
    in_specs, out_spec, parking = _pipelined_specs(n_tiles, seq // tm, tm, d, layer)
    return pl.pallas_call(
        functools.partial(_ffn_body, layer, half, sub, d_ff),
        grid=(n_tiles + 1,),
        in_specs=in_specs + [
            pl.BlockSpec(memory_space=pl.ANY),
            pl.BlockSpec(memory_space=pl.ANY),
            _resident((None, 3, d), lambda i: (layer, 0, 0)),
            _resident((None, 3, d), lambda i: (layer, 0, 0)),
        ],
        out_specs=out_spec,
        out_shape=jax.ShapeDtypeStruct((m, d), F32),
        scratch_shapes=[
            pltpu.VMEM((d, 2 * d_ff), BF16),
            pltpu.VMEM((d_ff, d), BF16),
            pltpu.VMEM((FFN_WEIGHT_SLOTS, 2, d, V7X_MXU_N), F32),
            pltpu.VMEM((FFN_WEIGHT_SLOTS, V7X_MXU_N, d), F32),
            pltpu.SemaphoreType.DMA((FFN_WEIGHT_SLOTS, 3)),
            pltpu.VMEM((tm, d_ff), BF16),
        ] + parking,
        compiler_params=_params(1),
        name="ffn",
    )(x, mod, w_gu, w_down, ln_g, ln_b)


def _gmlp_body(x_ref, mod_ref, win_ref, bin_ref, glng_ref, glnb_ref, ws_ref, bst_ref, wout_ref,
               lng_ref, lnb_ref, o_ref, v_ref, m_ref):
    tm = x_ref.shape[0]
    half = wout_ref.shape[0]
    gdim = half // GMLP_GROUPS
    x = x_ref[...]
    h = _modulate(x, mod_ref, 1).astype(BF16)

    def project(lo):
        return jnp.dot(h, win_ref[:, lo:lo + gdim], preferred_element_type=F32) + bin_ref[:, lo:lo + gdim]

    order = [half + g * gdim for g in range(GMLP_GROUPS)] + [g * gdim for g in range(GMLP_GROUPS)]
    pending = [project(lo) for lo in order[:GMLP_DOTS_AHEAD]]

    def next_projection(n):
        z = pending.pop(0)
        if n + GMLP_DOTS_AHEAD < len(order):
            pending.append(project(order[n + GMLP_DOTS_AHEAD]))
        return z

    for g in range(GMLP_GROUPS):
        v_ref[:, g * gdim:(g + 1) * gdim] = _gelu(next_projection(g))
    v = v_ref[...]
    mu = jnp.mean(v, axis=-1, keepdims=True)
    dv = v - mu
    rstd = lax.rsqrt(jnp.mean(dv * dv, axis=-1, keepdims=True) + LN_EPS)

    t_out = lax.broadcasted_iota(jnp.int32, (GMLP_WINDOW, GMLP_WINDOW), 0)
    s_in = lax.broadcasted_iota(jnp.int32, (GMLP_WINDOW, GMLP_WINDOW), 1)
    causal = (s_in // CHUNK) <= (t_out // CHUNK)

    for g in range(GMLP_GROUPS):
        lo = g * gdim
        vn = ((v_ref[:, lo:lo + gdim] - mu) * rstd * glng_ref[:, lo:lo + gdim]
              + glnb_ref[:, lo:lo + gdim]).astype(BF16)
        ws_g = jnp.where(causal, ws_ref[g], 0.0).astype(BF16)
        b_col = bst_ref[:, g:g + 1]
        u = _gelu(next_projection(GMLP_GROUPS + g))
        for w in range(tm // GMLP_WINDOW):
            rows = slice(w * GMLP_WINDOW, (w + 1) * GMLP_WINDOW)
            s = jnp.dot(ws_g, vn[rows], preferred_element_type=F32) + b_col
            m_ref[rows, lo:lo + gdim] = (u[rows] * s).astype(BF16)

    y = jnp.dot(m_ref[...], wout_ref[...], preferred_element_type=F32)
    r = ALPHA * x + _gate(mod_ref, 1) * y
    o_ref[...] = _layer_norm(r, lng_ref[1:2, :], lnb_ref[1:2, :])


def _gmlp(x, mod, w_in, b_in, gln_g, gln_b, w_s, b_s_t, w_out, ln_g, ln_b, *, layer, seq, tm):
    m, d = x.shape
    width = w_in.shape[2]
    half = width // 2
    tiles_per_seq = seq // tm
    const = lambda i: (layer, 0, 0)
    return pl.pallas_call(
        _gmlp_body,
        grid=(m // tm,),
        in_specs=[
            pl.BlockSpec((tm, d), lambda i: (i, 0)),
            pl.BlockSpec((None, None, N_MOD, d), lambda i: (layer, i // tiles_per_seq, 0, 0)),
            _resident((None, d, width), const),
            _resident((None, 1, width), const),
            _resident((None, 1, half), const),
            _resident((None, 1, half), const),
            _resident((None, GMLP_GROUPS, GMLP_WINDOW, GMLP_WINDOW), lambda i: (layer, 0, 0, 0)),
            _resident((None, GMLP_WINDOW, GMLP_GROUPS), const),
            _resident((None, half, d), const),
            _resident((None, 3, d), const),
            _resident((None, 3, d), const),
        ],
        out_specs=pl.BlockSpec((tm, d), lambda i: (i, 0)),
        out_shape=jax.ShapeDtypeStruct((m, d), F32),
        scratch_shapes=[pltpu.VMEM((tm, half), F32), pltpu.VMEM((tm, half), BF16)],
        compiler_params=_params(1),
        name="gmlp",
    )(x, mod, w_in, b_in, gln_g, gln_b, w_s, b_s_t, w_out, ln_g, ln_b)


def _kv_body(x_ref, mkv_ref, wk_ref, wvt_ref, k_ref, vt_ref):
    i = pl.program_id(1)

    @pl.when(i == 0)
    def _():
        k_ref[...] = jnp.zeros_like(k_ref)
        vt_ref[...] = jnp.zeros_like(vt_ref)

    @pl.when(i > 0)
    def _():
        h = (x_ref[...] * (1.0 + mkv_ref[1:2, :]) + mkv_ref[0:1, :]).astype(BF16)
        k = jnp.dot(h, wk_ref[...], preferred_element_type=F32)
        k_ref[...] = k.astype(BF16).reshape(k_ref.shape)
        vt = lax.dot_general(wvt_ref[...], h, (((1,), (1,)), ((), ())), preferred_element_type=F32)
        for n in range(vt_ref.shape[0]):
            vt_ref[n] = vt[:, n * KV_BLOCK:(n + 1) * KV_BLOCK].astype(BF16)


def _kv_project(x, mkv, w_k, w_v_t):
    bsz, seq, d = x.shape
    tm = LEFT_PAD
    blocks_per_tile = tm // KV_BLOCK
    n_blocks = (LEFT_PAD + seq) // KV_BLOCK
    return pl.pallas_call(
        _kv_body,
        grid=(bsz, 1 + seq // tm),
        in_specs=[
            pl.BlockSpec((None, tm, d), lambda b, i: (b, jnp.maximum(i - 1, 0), 0)),
            pl.BlockSpec((None, 2, d), lambda b, i: (b, 0, 0)),
            _resident((d, d), lambda b, i: (0, 0)),
            _resident((d, d), lambda b, i: (0, 0)),
        ],
        out_specs=[
            pl.BlockSpec((None, blocks_per_tile, KV_BLOCK, d), lambda b, i: (b, i, 0, 0)),
            pl.BlockSpec((None, blocks_per_tile, d, KV_BLOCK), lambda b, i: (b, i, 0, 0)),
        ],
        out_shape=[
            jax.ShapeDtypeStruct((bsz, n_blocks, KV_BLOCK, d), BF16),
            jax.ShapeDtypeStruct((bsz, n_blocks, d, KV_BLOCK), BF16),
        ],
        compiler_params=_params(2),
        name="kv_project",
    )(x, mkv, w_k, w_v_t)


def _bias_body(rb_ref, o_ref):
    rows, tq = o_ref.shape[1], o_ref.shape[2]
    width = 2 * tq
    r0 = pl.program_id(0) * rows
    rb = rb_ref[...]
    hi = rb.astype(BF16)
    rem = rb - hi.astype(F32)
    mid = rem.astype(BF16)
    lo = (rem - mid.astype(F32)).astype(BF16)
    lane = lax.broadcasted_iota(jnp.int32, (1, width), 1)
    rel = lax.broadcasted_iota(jnp.int32, (N_REL, width), 0)
    idx = jnp.clip(lane - (rows - 1) - r0 + LEFT_PAD, -(CHUNK - 1), MAX_REL) + (CHUNK - 1)
    onehot = (rel == idx).astype(BF16)
    wide = (jnp.dot(hi, onehot, preferred_element_type=F32)
            + jnp.dot(mid, onehot, preferred_element_type=F32)
            + jnp.dot(lo, onehot, preferred_element_type=F32)) * LOG2E
    qc = lax.broadcasted_iota(jnp.int32, (1, tq), 1) // CHUNK
    for k in range(rows):
        kc = (r0 + k) // CHUNK
        visible = (kc >= qc) & (kc <= qc + LEFT_CHUNKS)
        o_ref[:, k, :] = jnp.where(visible, wide[:, rows - 1 - k:rows - 1 - k + tq], NEG_BIG)


def _bias_table(rel_bias, tq):
    nh = rel_bias.shape[0]
    tk = tq + LEFT_PAD
    rows = BIAS_ROWS_PER_STEP
    assert tq + rows - 1 <= 2 * tq and tk % rows == 0
    return pl.pallas_call(
        _bias_body,
        grid=(tk // rows,),
        in_specs=[pl.BlockSpec((nh, N_REL), lambda i: (0, 0))],
        out_specs=pl.BlockSpec((nh, rows, tq), lambda i: (0, i, 0)),
        out_shape=jax.ShapeDtypeStruct((nh, tk, tq), F32),
        compiler_params=_params(1),
        name="bias_table",
    )(rel_bias)


def _attn_step(mask_padding, x_ref, mod_ref, wq_ref, wo_ref, k_ref, vt_ref, bias_ref, lng_ref, lnb_ref,
               o_ref, ctxt_ref):
    tq, d = x_ref.shape
    n_sub = tq // KV_BLOCK
    win_blocks = (KV_BLOCK + LEFT_PAD) // KV_BLOCK
    tk = win_blocks * KV_BLOCK
    sub_k = ATTN_SUB_Q + LEFT_PAD
    hd = d // N_HEADS
    first_block = pl.program_id(1) * n_sub
    x = x_ref[...]
    h = _modulate(x, mod_ref, 1)
    q = jnp.dot(h.astype(BF16), wq_ref[...], preferred_element_type=F32) * (hd ** -0.5 * LOG2E)
    q = q.astype(BF16)

    first_head = lax.broadcasted_iota(jnp.int32, (KV_BLOCK, V7X_LANES), 1) < hd
    ones_rows = jnp.ones((V7X_BF16_SUBLANES, KV_BLOCK), BF16)

    def scores_t(sub, head):
        lanes = slice(head // 2 * V7X_LANES, (head // 2 + 1) * V7X_LANES)
        q_pair = q[sub * KV_BLOCK:(sub + 1) * KV_BLOCK, lanes]
        k_win = k_ref[pl.ds(first_block + sub, win_blocks), :, lanes].reshape(tk, V7X_LANES)
        q_one = jnp.where(first_head if head % 2 == 0 else ~first_head, q_pair, jnp.zeros_like(q_pair))
        return lax.dot_general(k_win, q_one, (((1,), (1,)), ((), ())), preferred_element_type=F32)

    def softmax_t(sub, head, s_t):
        first_key_row = LEFT_PAD - (first_block + sub) * KV_BLOCK
        probs = []
        for r0 in range(0, KV_BLOCK, ATTN_SUB_Q):
            s_sub = s_t[r0:r0 + sub_k, r0:r0 + ATTN_SUB_Q] + bias_ref[head]
            if mask_padding:
                row = lax.broadcasted_iota(jnp.int32, (sub_k, 1), 0)
                s_sub = jnp.where(row >= first_key_row - r0, s_sub, NEG_BIG)
            s_max = jnp.max(s_sub, axis=0, keepdims=True)
            pieces = [jnp.exp2(s_sub - s_max).astype(BF16)]
            if r0:
                pieces.insert(0, jnp.zeros((r0, ATTN_SUB_Q), BF16))
            if tk - sub_k - r0:
                pieces.append(jnp.zeros((tk - sub_k - r0, ATTN_SUB_Q), BF16))
            probs.append(jnp.concatenate(pieces, axis=0))
        return jnp.concatenate(probs, axis=1)

    def context_t(sub, head, p_t):
        rows = slice(head * hd, (head + 1) * hd)
        acc = None
        for n in range(win_blocks):
            v_t = vt_ref[pl.ds(first_block + sub + n, 1), rows, :].reshape(hd, KV_BLOCK)
            lhs = jnp.concatenate([v_t, ones_rows], axis=0)
            part = jnp.dot(lhs, p_t[n * KV_BLOCK:(n + 1) * KV_BLOCK], preferred_element_type=F32)
            acc = part if acc is None else acc + part
        ctxt_ref[rows, sub * KV_BLOCK:(sub + 1) * KV_BLOCK] = (acc[:hd] / acc[hd:hd + 1]).astype(BF16)

    items = [(sub, head) for sub in range(n_sub) for head in range(N_HEADS)]
    pending = [scores_t(*item) for item in items[:ATTN_HEADS_AHEAD]]
    for n, item in enumerate(items):
        s_cur = pending.pop(0)
        if n + ATTN_HEADS_AHEAD < len(items):
            pending.append(scores_t(*items[n + ATTN_HEADS_AHEAD]))
        context_t(*item, softmax_t(*item, s_cur))

    y = lax.dot_general(ctxt_ref[...], wo_ref[...], (((0,), (0,)), ((), ())), preferred_element_type=F32)
    r = ALPHA * x + _gate(mod_ref, 1) * y
    o_ref[...] = _layer_norm(r, lng_ref[1:2, :], lnb_ref[1:2, :])


def _attn_body(x_ref, *refs):
    has_padding = pl.program_id(1) * x_ref.shape[0] < LEFT_PAD
    pl.when(has_padding)(functools.partial(_attn_step, True, x_ref, *refs))
    pl.when(jnp.logical_not(has_padding))(functools.partial(_attn_step, False, x_ref, *refs))


def _attention(x, mod, w_q, w_o, k_pad, vt_pad, bias, ln_g, ln_b, *, layer, j):
    bsz, seq, d = x.shape
    tq = ATTN_TQ
    kv_blocks = k_pad.shape[1]
    return pl.pallas_call(
        _attn_body,
        grid=(bsz, seq // tq),
        in_specs=[
            pl.BlockSpec((None, tq, d), lambda b, i: (b, i, 0)),
            pl.BlockSpec((None, None, N_MOD, d), lambda b, i: (layer, b, 0, 0)),
            _resident((None, d, d), lambda b, i: (j, 0, 0)),
            _resident((None, d, d), lambda b, i: (j, 0, 0)),
            pl.BlockSpec((None, kv_blocks, KV_BLOCK, d), lambda b, i: (b, 0, 0, 0)),
            pl.BlockSpec((None, kv_blocks, d, KV_BLOCK), lambda b, i: (b, 0, 0, 0)),
            _resident((N_HEADS, ATTN_SUB_Q + LEFT_PAD, ATTN_SUB_Q), lambda b, i: (j, 0, 0)),
            _resident((None, 3, d), lambda b, i: (layer, 0, 0)),
            _resident((None, 3, d), lambda b, i: (layer, 0, 0)),
        ],
        out_specs=pl.BlockSpec((None, tq, d), lambda b, i: (b, i, 0)),
        out_shape=jax.ShapeDtypeStruct((bsz, seq, d), F32),
        scratch_shapes=[pltpu.VMEM((d, tq), BF16)],
        compiler_params=_params(2),
        name="attention",
    )(x, mod, w_q, w_o, k_pad, vt_pad, bias, ln_g, ln_b)


def kernel(x, c, w_ada, b_ada, ln_g, ln_b, ffn_gu, ffn_down, gmlp_w_in, gmlp_b_in, gmlp_ln_g, gmlp_ln_b,
           gmlp_w_s, gmlp_b_s, gmlp_w_out, w_ada_kv, b_ada_kv, w_kv, attn_w_q, attn_rel_bias, attn_w_o):
    bsz, seq, d = x.shape
    mod = _adaln(c, w_ada, b_ada, ADALN_TN).reshape(DEPTH, bsz, N_MOD, d)
    mkv = _adaln(c, w_ada_kv[None], b_ada_kv[None], w_ada_kv.shape[1]).reshape(bsz, 2, d)

    w_in_b = gmlp_w_in.astype(BF16)
    w_out_b = gmlp_w_out.astype(BF16)
    w_k_b = w_kv[:, :d].astype(BF16)
    w_v_t_b = w_kv[:, d:].T.astype(BF16)
    w_q_b = attn_w_q.astype(BF16)
    w_o_b = attn_w_o.astype(BF16)
    b_in = gmlp_b_in[:, None, :]
    gln_g = gmlp_ln_g[:, None, :]
    gln_b = gmlp_ln_b[:, None, :]
    b_s_t = jnp.swapaxes(gmlp_b_s, 1, 2)
    bias = _bias_table(attn_rel_bias.reshape(N_B * N_HEADS, N_REL), ATTN_SUB_Q)

    xf = x.reshape(bsz * seq, d)
    k_pad = vt_pad = None
    for l in range(DEPTH):
        ffn = functools.partial(_ffn, mod=mod, w_gu=ffn_gu, w_down=ffn_down, ln_g=ln_g, ln_b=ln_b,
                                layer=l, seq=seq, tm=FFN_TM)
        xf = ffn(xf, sub=0, half=0)
        if l < N_A:
            xf = _gmlp(xf, mod, w_in_b, b_in, gln_g, gln_b, gmlp_w_s, b_s_t, w_out_b, ln_g, ln_b,
                       layer=l, seq=seq, tm=GMLP_TM)
        else:
            xf = _attention(xf.reshape(bsz, seq, d), mod, w_q_b, w_o_b, k_pad, vt_pad, bias, ln_g, ln_b,
                            layer=l, j=l - N_A).reshape(bsz * seq, d)
        xf = ffn(xf, sub=2, half=1)
        if l == N_A - 1:
            k_pad, vt_pad = _kv_project(xf.reshape(bsz, seq, d), mkv, w_k_b, w_v_t_b)
    return xf.reshape(bsz, seq, d)
```

```python
import functools

import jax
import jax.numpy as jnp
from jax import lax
from jax.experimental import pallas as pl
from jax.experimental.pallas import tpu as pltpu

DEPTH = 4
CHUNK = 64
N_A = DEPTH // 2
N_B = DEPTH - N_A
GMLP_WINDOW = 128
GMLP_GROUPS = 8
N_HEADS = 16
LEFT_CHUNKS = 8
LEFT_PAD = LEFT_CHUNKS * CHUNK
MAX_REL = 4 * CHUNK
N_REL = (CHUNK - 1) + MAX_REL + 1
ALPHA = (2.0 * DEPTH) ** 0.25
LN_EPS = 1e-5
N_MOD = 9

V7X_LANES = 128
V7X_MXU_N = 256
V7X_BF16_SUBLANES = 16
V7X_VMEM_LIMIT_BYTES = 56 * 1024 * 1024

NEG_BIG = -1e30
LOG2E = 1.4426950408889634

KV_BLOCK = 256
ATTN_TQ = 512
ATTN_SUB_Q = 128
ATTN_HEADS_AHEAD = 2

FFN_TM = 512
FFN_FINISH_BLOCKS = 8
FFN_FIRST_FINISH_CHUNK = 1
FFN_WEIGHT_SLOTS = 3
GMLP_TM = 512
GMLP_DOTS_AHEAD = 2
ADALN_TN = 3072
BIAS_ROWS_PER_STEP = 64

F32 = jnp.float32
BF16 = jnp.bfloat16


def _params(n_axes):
    return pltpu.CompilerParams(
        dimension_semantics=("arbitrary",) * n_axes,
        vmem_limit_bytes=V7X_VMEM_LIMIT_BYTES,
    )


def _resident(block_shape, index_map):
    return pl.BlockSpec(block_shape, index_map, pipeline_mode=pl.Buffered(1))


def _layer_norm(r, g, b):
    mu = jnp.mean(r, axis=-1, keepdims=True)
    d = r - mu
    var = jnp.mean(d * d, axis=-1, keepdims=True)
    return d * lax.rsqrt(var + LN_EPS) * g + b


def _gelu(z):
    return 0.5 * z * (1.0 + lax.erf(z * (0.5 ** 0.5)))


def _modulate(x, mod_ref, sub):
    shift = mod_ref[3 * sub:3 * sub + 1, :]
    scale = mod_ref[3 * sub + 1:3 * sub + 2, :]
    return x * (1.0 + scale) + shift


def _gate(mod_ref, sub):
    return 1.0 + mod_ref[3 * sub + 2:3 * sub + 3, :]


def _zero_bits_of(v):
    bits = lax.bitcast_convert_type(v, jnp.int32)
    sixteen = jnp.full(bits.shape, 16, jnp.int32)
    cleared = lax.shift_right_logical(lax.shift_right_logical(bits, sixteen), sixteen)
    return jnp.max(cleared, axis=0, keepdims=True)


def _after(v, finished_rows):
    zero = _zero_bits_of(finished_rows)[:, :v.shape[1]]
    return jnp.where(zero == 0, v, jnp.zeros_like(v))


def _finish_rows(sub, n_blocks, lng_ref, lnb_ref, r_ref, o_ref, k):
    n = o_ref.shape[0] // n_blocks
    rows = slice(k * n, (k + 1) * n)
    out = _layer_norm(r_ref[rows, :], lng_ref[sub:sub + 1, :], lnb_ref[sub:sub + 1, :])
    o_ref[rows, :] = out
    return out


def _pipelined_specs(n_tiles, tiles_per_seq, tm, d, layer):
    cur = lambda i: jnp.minimum(i, n_tiles - 1)
    prev = lambda i: jnp.maximum(i - 1, 0)
    in_specs = [
        pl.BlockSpec((tm, d), lambda i: (cur(i), 0)),
        pl.BlockSpec((None, None, N_MOD, d), lambda i: (layer, cur(i) // tiles_per_seq, 0, 0)),
    ]
    out_spec = pl.BlockSpec((tm, d), lambda i: (prev(i), 0))
    parking = [pltpu.VMEM((tm, d), F32)]
    return in_specs, out_spec, parking


def _adaln_body(c_ref, w_ref, b_ref, o_ref):
    c = c_ref[...]
    c_act = c * jax.nn.sigmoid(c)
    o_ref[...] = jnp.dot(c_act, w_ref[...], preferred_element_type=F32) + b_ref[...]


def _adaln(c, w, b, tn):
    n_l, d, n = w.shape
    bsz = c.shape[0]
    return pl.pallas_call(
        _adaln_body,
        grid=(n_l, n // tn),
        in_specs=[
            pl.BlockSpec((bsz, d), lambda l, j: (0, 0)),
            pl.BlockSpec((None, d, tn), lambda l, j: (l, 0, j)),
            pl.BlockSpec((None, 1, tn), lambda l, j: (l, 0, j)),
        ],
        out_specs=pl.BlockSpec((None, bsz, tn), lambda l, j: (l, 0, j)),
        out_shape=jax.ShapeDtypeStruct((n_l, bsz, n), F32),
        compiler_params=_params(2),
        name="adaln",
    )(c, w, b.reshape(n_l, 1, n))


def _ffn_weight_copies(layer, half, d_ff, wgu_hbm, wd_hbm, stage_gu_ref, stage_d_ref, sem_ref, j):
    lo = j * V7X_MXU_N
    slot = j % FFN_WEIGHT_SLOTS
    return (
        pltpu.make_async_copy(wgu_hbm.at[layer, half, :, pl.ds(lo, V7X_MXU_N)],
                              stage_gu_ref.at[slot, 0], sem_ref.at[slot, 0]),
        pltpu.make_async_copy(wgu_hbm.at[layer, half, :, pl.ds(d_ff + lo, V7X_MXU_N)],
                              stage_gu_ref.at[slot, 1], sem_ref.at[slot, 1]),
        pltpu.make_async_copy(wd_hbm.at[layer, half, pl.ds(lo, V7X_MXU_N), :],
                              stage_d_ref.at[slot], sem_ref.at[slot, 2]),
    )


def _ffn_take_chunk(copies, n_chunks, d_ff, wgu_ref, wd_ref, stage_gu_ref, stage_d_ref, j):
    for copy in copies(j):
        copy.wait()
    lo = j * V7X_MXU_N
    slot = j % FFN_WEIGHT_SLOTS
    wgu_ref[:, lo:lo + V7X_MXU_N] = stage_gu_ref[slot, 0].astype(BF16)
    wgu_ref[:, d_ff + lo:d_ff + lo + V7X_MXU_N] = stage_gu_ref[slot, 1].astype(BF16)
    wd_ref[lo:lo + V7X_MXU_N, :] = stage_d_ref[slot].astype(BF16)
    if j + FFN_WEIGHT_SLOTS < n_chunks:
        for copy in copies(j + FFN_WEIGHT_SLOTS):
            copy.start()


def _ffn_start(sub, d_ff, x_ref, mod_ref, wgu_ref, wd_ref, a_ref, r_ref, finish_rows=None, take_chunk=None):
    x = x_ref[...]
    h = _modulate(x, mod_ref, sub).astype(BF16)
    for j in range(d_ff // V7X_MXU_N):
        lo = j * V7X_MXU_N
        if take_chunk is not None:
            take_chunk(j)
        g = jnp.dot(h, wgu_ref[:, lo:lo + V7X_MXU_N], preferred_element_type=F32)
        u = jnp.dot(h, wgu_ref[:, d_ff + lo:d_ff + lo + V7X_MXU_N], preferred_element_type=F32)
        act = g * jax.nn.sigmoid(g) * u
        k = j - FFN_FIRST_FINISH_CHUNK
        if finish_rows is not None and 0 <= k < FFN_FINISH_BLOCKS:
            act = _after(act, finish_rows(k))
        a_ref[:, lo:lo + V7X_MXU_N] = act.astype(BF16)
    y = jnp.dot(a_ref[...], wd_ref[...], preferred_element_type=F32)
    r_ref[...] = ALPHA * x + (0.5 * _gate(mod_ref, sub)) * y


def _ffn_body(layer, half, sub, d_ff, x_ref, mod_ref, wgu_hbm, wd_hbm, lng_ref, lnb_ref, o_ref,
              wgu_ref, wd_ref, stage_gu_ref, stage_d_ref, sem_ref, a_ref, r_ref):
    i = pl.program_id(0)
    last = pl.num_programs(0) - 1
    n_chunks = d_ff // V7X_MXU_N
    start = functools.partial(_ffn_start, sub, d_ff, x_ref, mod_ref, wgu_ref, wd_ref, a_ref, r_ref)
    finish_rows = functools.partial(_finish_rows, sub, FFN_FINISH_BLOCKS, lng_ref, lnb_ref, r_ref, o_ref)
    copies = functools.partial(_ffn_weight_copies, layer, half, d_ff, wgu_hbm, wd_hbm, stage_gu_ref, stage_d_ref,
                               sem_ref)
    take_chunk = functools.partial(_ffn_take_chunk, copies, n_chunks, d_ff, wgu_ref, wd_ref, stage_gu_ref,
                                   stage_d_ref)

    @pl.when(i == 0)
    def _():
        for j in range(FFN_WEIGHT_SLOTS):
            for copy in copies(j):
                copy.start()
        start(take_chunk=take_chunk)

    @pl.when((i > 0) & (i < last))
    def _():
        start(finish_rows=finish_rows)

    @pl.when(i == last)
    def _():
        for k in range(FFN_FINISH_BLOCKS):
            finish_rows(k)


def _ffn(x, mod, w_gu, w_down, ln_g, ln_b, *, layer, sub, half, seq, tm):
    m, d = x.shape
    d_ff = w_down.shape[2]
    n_tiles = m // tm
    in_specs, out_spec, parking = _pipelined_specs(n_tiles, seq // tm, tm, d, layer)
    return pl.pallas_call(
        functools.partial(_ffn_body, layer, half, sub, d_ff),
        grid=(n_tiles + 1,),
        in_specs=in_specs + [
            pl.BlockSpec(memory_space=pl.ANY),
            pl.BlockSpec(memory_space=pl.ANY),
            _resident((None, 3, d), lambda i: (layer, 0, 0)),
            _resident((None, 3, d), lambda i: (layer, 0, 0)),
        ],
        out_specs=out_spec,
        out_shape=jax.ShapeDtypeStruct((m, d), F32),
        scratch_shapes=[
            pltpu.VMEM((d, 2 * d_ff), BF16),
            pltpu.VMEM((d_ff, d), BF16),
            pltpu.VMEM((FFN_WEIGHT_SLOTS, 2, d, V7X_MXU_N), F32),
            pltpu.VMEM((FFN_WEIGHT_SLOTS, V7X_MXU_N, d), F32),
            pltpu.SemaphoreType.DMA((FFN_WEIGHT_SLOTS, 3)),
            pltpu.VMEM((tm, d_ff), BF16),
        ] + parking,
        compiler_params=_params(1),
        name="ffn",
    )(x, mod, w_gu, w_down, ln_g, ln_b)


def _gmlp_body(x_ref, mod_ref, win_ref, bin_ref, glng_ref, glnb_ref, ws_ref, bst_ref, wout_ref,
               lng_ref, lnb_ref, o_ref, v_ref, m_ref):
    tm = x_ref.shape[0]
    half = wout_ref.shape[0]
    gdim = half // GMLP_GROUPS
    x = x_ref[...]
    h = _modulate(x, mod_ref, 1).astype(BF16)

    def project(lo):
        return jnp.dot(h, win_ref[:, lo:lo + gdim], preferred_element_type=F32) + bin_ref[:, lo:lo + gdim]

    order = [half + g * gdim for g in range(GMLP_GROUPS)] + [g * gdim for g in range(GMLP_GROUPS)]
    pending = [project(lo) for lo in order[:GMLP_DOTS_AHEAD]]

    def next_projection(n):
        z = pending.pop(0)
        if n + GMLP_DOTS_AHEAD < len(order):
            pending.append(project(order[n + GMLP_DOTS_AHEAD]))
        return z

    for g in range(GMLP_GROUPS):
        v_ref[:, g * gdim:(g + 1) * gdim] = _gelu(next_projection(g))
    v = v_ref[...]
    mu = jnp.mean(v, axis=-1, keepdims=True)
    dv = v - mu
    rstd = lax.rsqrt(jnp.mean(dv * dv, axis=-1, keepdims=True) + LN_EPS)

    t_out = lax.broadcasted_iota(jnp.int32, (GMLP_WINDOW, GMLP_WINDOW), 0)
    s_in = lax.broadcasted_iota(jnp.int32, (GMLP_WINDOW, GMLP_WINDOW), 1)
    causal = (s_in // CHUNK) <= (t_out // CHUNK)

    for g in range(GMLP_GROUPS):
        lo = g * gdim
        vn = ((v_ref[:, lo:lo + gdim] - mu) * rstd * glng_ref[:, lo:lo + gdim]
              + glnb_ref[:, lo:lo + gdim]).astype(BF16)
        ws_g = jnp.where(causal, ws_ref[g], 0.0).astype(BF16)
        b_col = bst_ref[:, g:g + 1]
        u = _gelu(next_projection(GMLP_GROUPS + g))
        for w in range(tm // GMLP_WINDOW):
            rows = slice(w * GMLP_WINDOW, (w + 1) * GMLP_WINDOW)
            s = jnp.dot(ws_g, vn[rows], preferred_element_type=F32) + b_col
            m_ref[rows, lo:lo + gdim] = (u[rows] * s).astype(BF16)

    y = jnp.dot(m_ref[...], wout_ref[...], preferred_element_type=F32)
    r = ALPHA * x + _gate(mod_ref, 1) * y
    o_ref[...] = _layer_norm(r, lng_ref[1:2, :], lnb_ref[1:2, :])


def _gmlp(x, mod, w_in, b_in, gln_g, gln_b, w_s, b_s_t, w_out, ln_g, ln_b, *, layer, seq, tm):
    m, d = x.shape
    width = w_in.shape[2]
    half = width // 2
    tiles_per_seq = seq // tm
    const = lambda i: (layer, 0, 0)
    return pl.pallas_call(
        _gmlp_body,
        grid=(m // tm,),
        in_specs=[
            pl.BlockSpec((tm, d), lambda i: (i, 0)),
            pl.BlockSpec((None, None, N_MOD, d), lambda i: (layer, i // tiles_per_seq, 0, 0)),
            _resident((None, d, width), const),
            _resident((None, 1, width), const),
            _resident((None, 1, half), const),
            _resident((None, 1, half), const),
            _resident((None, GMLP_GROUPS, GMLP_WINDOW, GMLP_WINDOW), lambda i: (layer, 0, 0, 0)),
            _resident((None, GMLP_WINDOW, GMLP_GROUPS), const),
            _resident((None, half, d), const),
            _resident((None, 3, d), const),
            _resident((None, 3, d), const),
        ],
        out_specs=pl.BlockSpec((tm, d), lambda i: (i, 0)),
        out_shape=jax.ShapeDtypeStruct((m, d), F32),
        scratch_shapes=[pltpu.VMEM((tm, half), F32), pltpu.VMEM((tm, half), BF16)],
        compiler_params=_params(1),
        name="gmlp",
    )(x, mod, w_in, b_in, gln_g, gln_b, w_s, b_s_t, w_out, ln_g, ln_b)


def _kv_body(x_ref, mkv_ref, wk_ref, wvt_ref, k_ref, vt_ref):
    i = pl.program_id(1)

    @pl.when(i == 0)
    def _():
        k_ref[...] = jnp.zeros_like(k_ref)
        vt_ref[...] = jnp.zeros_like(vt_ref)

    @pl.when(i > 0)
    def _():
        h = (x_ref[...] * (1.0 + mkv_ref[1:2, :]) + mkv_ref[0:1, :]).astype(BF16)
        k = jnp.dot(h, wk_ref[...], preferred_element_type=F32)
        k_ref[...] = k.astype(BF16).reshape(k_ref.shape)
        vt = lax.dot_general(wvt_ref[...], h, (((1,), (1,)), ((), ())), preferred_element_type=F32)
        for n in range(vt_ref.shape[0]):
            vt_ref[n] = vt[:, n * KV_BLOCK:(n + 1) * KV_BLOCK].astype(BF16)


def _kv_project(x, mkv, w_k, w_v_t):
    bsz, seq, d = x.shape
    tm = LEFT_PAD
    blocks_per_tile = tm // KV_BLOCK
    n_blocks = (LEFT_PAD + seq) // KV_BLOCK
    return pl.pallas_call(
        _kv_body,
        grid=(bsz, 1 + seq // tm),
        in_specs=[
            pl.BlockSpec((None, tm, d), lambda b, i: (b, jnp.maximum(i - 1, 0), 0)),
            pl.BlockSpec((None, 2, d), lambda b, i: (b, 0, 0)),
            _resident((d, d), lambda b, i: (0, 0)),
            _resident((d, d), lambda b, i: (0, 0)),
        ],
        out_specs=[
            pl.BlockSpec((None, blocks_per_tile, KV_BLOCK, d), lambda b, i: (b, i, 0, 0)),
            pl.BlockSpec((None, blocks_per_tile, d, KV_BLOCK), lambda b, i: (b, i, 0, 0)),
        ],
        out_shape=[
            jax.ShapeDtypeStruct((bsz, n_blocks, KV_BLOCK, d), BF16),
            jax.ShapeDtypeStruct((bsz, n_blocks, d, KV_BLOCK), BF16),
        ],
        compiler_params=_params(2),
        name="kv_project",
    )(x, mkv, w_k, w_v_t)


def _bias_body(rb_ref, o_ref):
    rows, tq = o_ref.shape[1], o_ref.shape[2]
    width = 2 * tq
    r0 = pl.program_id(0) * rows
    rb = rb_ref[...]
    hi = rb.astype(BF16)
    rem = rb - hi.astype(F32)
    mid = rem.astype(BF16)
    lo = (rem - mid.astype(F32)).astype(BF16)
    lane = lax.broadcasted_iota(jnp.int32, (1, width), 1)
    rel = lax.broadcasted_iota(jnp.int32, (N_REL, width), 0)
    idx = jnp.clip(lane - (rows - 1) - r0 + LEFT_PAD, -(CHUNK - 1), MAX_REL) + (CHUNK - 1)
    onehot = (rel == idx).astype(BF16)
    wide = (jnp.dot(hi, onehot, preferred_element_type=F32)
            + jnp.dot(mid, onehot, preferred_element_type=F32)
            + jnp.dot(lo, onehot, preferred_element_type=F32)) * LOG2E
    qc = lax.broadcasted_iota(jnp.int32, (1, tq), 1) // CHUNK
    for k in range(rows):
        kc = (r0 + k) // CHUNK
        visible = (kc >= qc) & (kc <= qc + LEFT_CHUNKS)
        o_ref[:, k, :] = jnp.where(visible, wide[:, rows - 1 - k:rows - 1 - k + tq], NEG_BIG)


def _bias_table(rel_bias, tq):
    nh = rel_bias.shape[0]
    tk = tq + LEFT_PAD
    rows = BIAS_ROWS_PER_STEP
    assert tq + rows - 1 <= 2 * tq and tk % rows == 0
    return pl.pallas_call(
        _bias_body,
        grid=(tk // rows,),
        in_specs=[pl.BlockSpec((nh, N_REL), lambda i: (0, 0))],
        out_specs=pl.BlockSpec((nh, rows, tq), lambda i: (0, i, 0)),
        out_shape=jax.ShapeDtypeStruct((nh, tk, tq), F32),
        compiler_params=_params(1),
        name="bias_table",
    )(rel_bias)


def _attn_step(mask_padding, x_ref, mod_ref, wq_ref, wo_ref, k_ref, vt_ref, bias_ref, lng_ref, lnb_ref,
               o_ref, ctxt_ref):
    tq, d = x_ref.shape
    n_sub = tq // KV_BLOCK
    win_blocks = (KV_BLOCK + LEFT_PAD) // KV_BLOCK
    tk = win_blocks * KV_BLOCK
    sub_k = ATTN_SUB_Q + LEFT_PAD
    hd = d // N_HEADS
    first_block = pl.program_id(1) * n_sub
    x = x_ref[...]
    h = _modulate(x, mod_ref, 1)
    q = jnp.dot(h.astype(BF16), wq_ref[...], preferred_element_type=F32) * (hd ** -0.5 * LOG2E)
    q = q.astype(BF16)

    first_head = lax.broadcasted_iota(jnp.int32, (KV_BLOCK, V7X_LANES), 1) < hd
    ones_rows = jnp.ones((V7X_BF16_SUBLANES, KV_BLOCK), BF16)

    def scores_t(sub, head):
        lanes = slice(head // 2 * V7X_LANES, (head // 2 + 1) * V7X_LANES)
        q_pair = q[sub * KV_BLOCK:(sub + 1) * KV_BLOCK, lanes]
        k_win = k_ref[pl.ds(first_block + sub, win_blocks), :, lanes].reshape(tk, V7X_LANES)
        q_one = jnp.where(first_head if head % 2 == 0 else ~first_head, q_pair, jnp.zeros_like(q_pair))
        return lax.dot_general(k_win, q_one, (((1,), (1,)), ((), ())), preferred_element_type=F32)

    def softmax_t(sub, head, s_t):
        first_key_row = LEFT_PAD - (first_block + sub) * KV_BLOCK
        probs = []
        for r0 in range(0, KV_BLOCK, ATTN_SUB_Q):
            s_sub = s_t[r0:r0 + sub_k, r0:r0 + ATTN_SUB_Q] + bias_ref[head]
            if mask_padding:
                row = lax.broadcasted_iota(jnp.int32, (sub_k, 1), 0)
                s_sub = jnp.where(row >= first_key_row - r0, s_sub, NEG_BIG)
            s_max = jnp.max(s_sub, axis=0, keepdims=True)
            pieces = [jnp.exp2(s_sub - s_max).astype(BF16)]
            if r0:
                pieces.insert(0, jnp.zeros((r0, ATTN_SUB_Q), BF16))
            if tk - sub_k - r0:
                pieces.append(jnp.zeros((tk - sub_k - r0, ATTN_SUB_Q), BF16))
            probs.append(jnp.concatenate(pieces, axis=0))
        return jnp.concatenate(probs, axis=1)

    def context_t(sub, head, p_t):
        rows = slice(head * hd, (head + 1) * hd)
        acc = None
        for n in range(win_blocks):
            v_t = vt_ref[pl.ds(first_block + sub + n, 1), rows, :].reshape(hd, KV_BLOCK)
            lhs = jnp.concatenate([v_t, ones_rows], axis=0)
            part = jnp.dot(lhs, p_t[n * KV_BLOCK:(n + 1) * KV_BLOCK], preferred_element_type=F32)
            acc = part if acc is None else acc + part
        ctxt_ref[rows, sub * KV_BLOCK:(sub + 1) * KV_BLOCK] = (acc[:hd] / acc[hd:hd + 1]).astype(BF16)

    items = [(sub, head) for sub in range(n_sub) for head in range(N_HEADS)]
    pending = [scores_t(*item) for item in items[:ATTN_HEADS_AHEAD]]
    for n, item in enumerate(items):
        s_cur = pending.pop(0)
        if n + ATTN_HEADS_AHEAD < len(items):
            pending.append(scores_t(*items[n + ATTN_HEADS_AHEAD]))
        context_t(*item, softmax_t(*item, s_cur))

    y = lax.dot_general(ctxt_ref[...], wo_ref[...], (((0,), (0,)), ((), ())), preferred_element_type=F32)
    r = ALPHA * x + _gate(mod_ref, 1) * y
    o_ref[...] = _layer_norm(r, lng_ref[1:2, :], lnb_ref[1:2, :])


def _attn_body(x_ref, *refs):
    has_padding = pl.program_id(1) * x_ref.shape[0] < LEFT_PAD
    pl.when(has_padding)(functools.partial(_attn_step, True, x_ref, *refs))
    pl.when(jnp.logical_not(has_padding))(functools.partial(_attn_step, False, x_ref, *refs))


def _attention(x, mod, w_q, w_o, k_pad, vt_pad, bias, ln_g, ln_b, *, layer, j):
    bsz, seq, d = x.shape
    tq = ATTN_TQ
    kv_blocks = k_pad.shape[1]
    return pl.pallas_call(
        _attn_body,
        grid=(bsz, seq // tq),
        in_specs=[
            pl.BlockSpec((None, tq, d), lambda b, i: (b, i, 0)),
            pl.BlockSpec((None, None, N_MOD, d), lambda b, i: (layer, b, 0, 0)),
            _resident((None, d, d), lambda b, i: (j, 0, 0)),
            _resident((None, d, d), lambda b, i: (j, 0, 0)),
            pl.BlockSpec((None, kv_blocks, KV_BLOCK, d), lambda b, i: (b, 0, 0, 0)),
            pl.BlockSpec((None, kv_blocks, d, KV_BLOCK), lambda b, i: (b, 0, 0, 0)),
            _resident((N_HEADS, ATTN_SUB_Q + LEFT_PAD, ATTN_SUB_Q), lambda b, i: (j, 0, 0)),
            _resident((None, 3, d), lambda b, i: (layer, 0, 0)),
            _resident((None, 3, d), lambda b, i: (layer, 0, 0)),
        ],
        out_specs=pl.BlockSpec((None, tq, d), lambda b, i: (b, i, 0)),
        out_shape=jax.ShapeDtypeStruct((bsz, seq, d), F32),
        scratch_shapes=[pltpu.VMEM((d, tq), BF16)],
        compiler_params=_params(2),
        name="attention",
    )(x, mod, w_q, w_o, k_pad, vt_pad, bias, ln_g, ln_b)


def kernel(x, c, w_ada, b_ada, ln_g, ln_b, ffn_gu, ffn_down, gmlp_w_in, gmlp_b_in, gmlp_ln_g, gmlp_ln_b,
           gmlp_w_s, gmlp_b_s, gmlp_w_out, w_ada_kv, b_ada_kv, w_kv, attn_w_q, attn_rel_bias, attn_w_o):
    bsz, seq, d = x.shape
    mod = _adaln(c, w_ada, b_ada, ADALN_TN).reshape(DEPTH, bsz, N_MOD, d)
    mkv = _adaln(c, w_ada_kv[None], b_ada_kv[None], w_ada_kv.shape[1]).reshape(bsz, 2, d)

    w_in_b = gmlp_w_in.astype(BF16)
    w_out_b = gmlp_w_out.astype(BF16)
    w_k_b = w_kv[:, :d].astype(BF16)
    w_v_t_b = w_kv[:, d:].T.astype(BF16)
    w_q_b = attn_w_q.astype(BF16)
    w_o_b = attn_w_o.astype(BF16)
    b_in = gmlp_b_in[:, None, :]
    gln_g = gmlp_ln_g[:, None, :]
    gln_b = gmlp_ln_b[:, None, :]
    b_s_t = jnp.swapaxes(gmlp_b_s, 1, 2)
    bias = _bias_table(attn_rel_bias.reshape(N_B * N_HEADS, N_REL), ATTN_SUB_Q)

    xf = x.reshape(bsz * seq, d)
    k_pad = vt_pad = None
    for l in range(DEPTH):
        ffn = functools.partial(_ffn, mod=mod, w_gu=ffn_gu, w_down=ffn_down, ln_g=ln_g, ln_b=ln_b,
                                layer=l, seq=seq, tm=FFN_TM)
        xf = ffn(xf, sub=0, half=0)
        if l < N_A:
            xf = _gmlp(xf, mod, w_in_b, b_in, gln_g, gln_b, gmlp_w_s, b_s_t, w_out_b, ln_g, ln_b,
                       layer=l, seq=seq, tm=GMLP_TM)
        else:
            xf = _attention(xf.reshape(bsz, seq, d), mod, w_q_b, w_o_b, k_pad, vt_pad, bias, ln_g, ln_b,
                            layer=l, j=l - N_A).reshape(bsz * seq, d)
        xf = ffn(xf, sub=2, half=1)
        if l == N_A - 1:
            k_pad, vt_pad = _kv_project(xf.reshape(bsz, seq, d), mkv, w_k_b, w_v_t_b)
    return xf.reshape(bsz, seq, d)
```

```python
import functools

import jax
import jax.numpy as jnp
from jax import lax
from jax.experimental import pallas as pl
from jax.experimental.pallas import tpu as pltpu

DEPTH = 4
CHUNK = 64
N_A = DEPTH // 2
N_B = DEPTH - N_A
GMLP_WINDOW = 128
GMLP_GROUPS = 8
N_HEADS = 16
LEFT_CHUNKS = 8
LEFT_PAD = LEFT_CHUNKS * CHUNK
MAX_REL = 4 * CHUNK
N_REL = (CHUNK - 1) + MAX_REL + 1
ALPHA = (2.0 * DEPTH) ** 0.25
LN_EPS = 1e-5
N_MOD = 9

V7X_LANES = 128
V7X_MXU_N = 256
V7X_BF16_SUBLANES = 16
V7X_VMEM_LIMIT_BYTES = 56 * 1024 * 1024

NEG_BIG = -1e30
LOG2E = 1.4426950408889634

KV_BLOCK = 256
ATTN_TQ = 512
ATTN_SUB_Q = 128
ATTN_HEADS_AHEAD = 2

FFN_TM = 512
FFN_FINISH_BLOCKS = 8
FFN_FIRST_FINISH_CHUNK = 1
FFN_WEIGHT_SLOTS = 3
GMLP_TM = 512
GMLP_DOTS_AHEAD = 2
ADALN_TN = 3072
BIAS_ROWS_PER_STEP = 64

F32 = jnp.float32
BF16 = jnp.bfloat16


def _params(n_axes):
    return pltpu.CompilerParams(
        dimension_semantics=("arbitrary",) * n_axes,
        vmem_limit_bytes=V7X_VMEM_LIMIT_BYTES,
    )


def _resident(block_shape, index_map):
    return pl.BlockSpec(block_shape, index_map, pipeline_mode=pl.Buffered(1))


def _layer_norm(r, g, b):
    mu = jnp.mean(r, axis=-1, keepdims=True)
    d = r - mu
    var = jnp.mean(d * d, axis=-1, keepdims=True)
    return d * lax.rsqrt(var + LN_EPS) * g + b


def _gelu(z):
    return 0.5 * z * (1.0 + lax.erf(z * (0.5 ** 0.5)))


def _modulate(x, mod_ref, sub):
    shift = mod_ref[3 * sub:3 * sub + 1, :]
    scale = mod_ref[3 * sub + 1:3 * sub + 2, :]
    return x * (1.0 + scale) + shift


def _gate(mod_ref, sub):
    return 1.0 + mod_ref[3 * sub + 2:3 * sub + 3, :]


def _zero_bits_of(v):
    bits = lax.bitcast_convert_type(v, jnp.int32)
    sixteen = jnp.full(bits.shape, 16, jnp.int32)
    cleared = lax.shift_right_logical(lax.shift_right_logical(bits, sixteen), sixteen)
    return jnp.max(cleared, axis=0, keepdims=True)


def _after(v, finished_rows):
    zero = _zero_bits_of(finished_rows)[:, :v.shape[1]]
    return jnp.where(zero == 0, v, jnp.zeros_like(v))


def _finish_rows(sub, n_blocks, lng_ref, lnb_ref, r_ref, o_ref, k):
    n = o_ref.shape[0] // n_blocks
    rows = slice(k * n, (k + 1) * n)
    out = _layer_norm(r_ref[rows, :], lng_ref[sub:sub + 1, :], lnb_ref[sub:sub + 1, :])
    o_ref[rows, :] = out
    return out


def _pipelined_specs(n_tiles, tiles_per_seq, tm, d, layer):
    cur = lambda i: jnp.minimum(i, n_tiles - 1)
    prev = lambda i: jnp.maximum(i - 1, 0)
    in_specs = [
        pl.BlockSpec((tm, d), lambda i: (cur(i), 0)),
        pl.BlockSpec((None, None, N_MOD, d), lambda i: (layer, cur(i) // tiles_per_seq, 0, 0)),
    ]
    out_spec = pl.BlockSpec((tm, d), lambda i: (prev(i), 0))
    parking = [pltpu.VMEM((tm, d), F32)]
    return in_specs, out_spec, parking


def _adaln_body(c_ref, w_ref, b_ref, o_ref):
    c = c_ref[...]
    c_act = c * jax.nn.sigmoid(c)
    o_ref[...] = jnp.dot(c_act, w_ref[...], preferred_element_type=F32) + b_ref[...]


def _adaln(c, w, b, tn):
    n_l, d, n = w.shape
    bsz = c.shape[0]
    return pl.pallas_call(
        _adaln_body,
        grid=(n_l, n // tn),
        in_specs=[
            pl.BlockSpec((bsz, d), lambda l, j: (0, 0)),
            pl.BlockSpec((None, d, tn), lambda l, j: (l, 0, j)),
            pl.BlockSpec((None, 1, tn), lambda l, j: (l, 0, j)),
        ],
        out_specs=pl.BlockSpec((None, bsz, tn), lambda l, j: (l, 0, j)),
        out_shape=jax.ShapeDtypeStruct((n_l, bsz, n), F32),
        compiler_params=_params(2),
        name="adaln",
    )(c, w, b.reshape(n_l, 1, n))


def _ffn_weight_copies(layer, half, d_ff, wgu_hbm, wd_hbm, stage_gu_ref, stage_d_ref, sem_ref, j):
    lo = j * V7X_MXU_N
    slot = j % FFN_WEIGHT_SLOTS
    return (
        pltpu.make_async_copy(wgu_hbm.at[layer, half, :, pl.ds(lo, V7X_MXU_N)],
                              stage_gu_ref.at[slot, 0], sem_ref.at[slot, 0]),
        pltpu.make_async_copy(wgu_hbm.at[layer, half, :, pl.ds(d_ff + lo, V7X_MXU_N)],
                              stage_gu_ref.at[slot, 1], sem_ref.at[slot, 1]),
        pltpu.make_async_copy(wd_hbm.at[layer, half, pl.ds(lo, V7X_MXU_N), :],
                              stage_d_ref.at[slot], sem_ref.at[slot, 2]),
    )


def _ffn_take_chunk(copies, n_chunks, d_ff, wgu_ref, wd_ref, stage_gu_ref, stage_d_ref, j):
    for copy in copies(j):
        copy.wait()
    lo = j * V7X_MXU_N
    slot = j % FFN_WEIGHT_SLOTS
    wgu_ref[:, lo:lo + V7X_MXU_N] = stage_gu_ref[slot, 0].astype(BF16)
    wgu_ref[:, d_ff + lo:d_ff + lo + V7X_MXU_N] = stage_gu_ref[slot, 1].astype(BF16)
    wd_ref[lo:lo + V7X_MXU_N, :] = stage_d_ref[slot].astype(BF16)
    if j + FFN_WEIGHT_SLOTS < n_chunks:
        for copy in copies(j + FFN_WEIGHT_SLOTS):
            copy.start()


def _ffn_start(sub, d_ff, x_ref, mod_ref, wgu_ref, wd_ref, a_ref, r_ref, finish_rows=None, take_chunk=None):
    x = x_ref[...]
    h = _modulate(x, mod_ref, sub).astype(BF16)
    for j in range(d_ff // V7X_MXU_N):
        lo = j * V7X_MXU_N
        if take_chunk is not None:
            take_chunk(j)
        g = jnp.dot(h, wgu_ref[:, lo:lo + V7X_MXU_N], preferred_element_type=F32)
        u = jnp.dot(h, wgu_ref[:, d_ff + lo:d_ff + lo + V7X_MXU_N], preferred_element_type=F32)
        act = g * jax.nn.sigmoid(g) * u
        k = j - FFN_FIRST_FINISH_CHUNK
        if finish_rows is not None and 0 <= k < FFN_FINISH_BLOCKS:
            act = _after(act, finish_rows(k))
        a_ref[:, lo:lo + V7X_MXU_N] = act.astype(BF16)
    y = jnp.dot(a_ref[...], wd_ref[...], preferred_element_type=F32)
    r_ref[...] = ALPHA * x + (0.5 * _gate(mod_ref, sub)) * y


def _ffn_body(layer, half, sub, d_ff, x_ref, mod_ref, wgu_hbm, wd_hbm, lng_ref, lnb_ref, o_ref,
              wgu_ref, wd_ref, stage_gu_ref, stage_d_ref, sem_ref, a_ref, r_ref):
    i = pl.program_id(0)
    last = pl.num_programs(0) - 1
    n_chunks = d_ff // V7X_MXU_N
    start = functools.partial(_ffn_start, sub, d_ff, x_ref, mod_ref, wgu_ref, wd_ref, a_ref, r_ref)
    finish_rows = functools.partial(_finish_rows, sub, FFN_FINISH_BLOCKS, lng_ref, lnb_ref, r_ref, o_ref)
    copies = functools.partial(_ffn_weight_copies, layer, half, d_ff, wgu_hbm, wd_hbm, stage_gu_ref, stage_d_ref,
                               sem_ref)
    take_chunk = functools.partial(_ffn_take_chunk, copies, n_chunks, d_ff, wgu_ref, wd_ref, stage_gu_ref,
                                   stage_d_ref)

    @pl.when(i == 0)
    def _():
        for j in range(FFN_WEIGHT_SLOTS):
            for copy in copies(j):
                copy.start()
        start(take_chunk=take_chunk)

    @pl.when((i > 0) & (i < last))
    def _():
        start(finish_rows=finish_rows)

    @pl.when(i == last)
    def _():
        for k in range(FFN_FINISH_BLOCKS):
            finish_rows(k)


def _ffn(x, mod, w_gu, w_down, ln_g, ln_b, *, layer, sub, half, seq, tm):
    m, d = x.shape
    d_ff = w_down.shape[2]
    n_tiles = m // tm
    in_specs, out_spec, parking = _pipelined_specs(n_tiles, seq // tm, tm, d, layer)
    return pl.pallas_call(
        functools.partial(_ffn_body, layer, half, sub, d_ff),
        grid=(n_tiles + 1,),
        in_specs=in_specs + [
            pl.BlockSpec(memory_space=pl.ANY),
            pl.BlockSpec(memory_space=pl.ANY),
            _resident((None, 3, d), lambda i: (layer, 0, 0)),
            _resident((None, 3, d), lambda i: (layer, 0, 0)),
        ],
        out_specs=out_spec,
        out_shape=jax.ShapeDtypeStruct((m, d), F32),
        scratch_shapes=[
            pltpu.VMEM((d, 2 * d_ff), BF16),
            pltpu.VMEM((d_ff, d), BF16),
            pltpu.VMEM((FFN_WEIGHT_SLOTS, 2, d, V7X_MXU_N), F32),
            pltpu.VMEM((FFN_WEIGHT_SLOTS, V7X_MXU_N, d), F32),
            pltpu.SemaphoreType.DMA((FFN_WEIGHT_SLOTS, 3)),
            pltpu.VMEM((tm, d_ff), BF16),
        ] + parking,
        compiler_params=_params(1),
        name="ffn",
    )(x, mod, w_gu, w_down, ln_g, ln_b)


def _gmlp_body(x_ref, mod_ref, win_ref, bin_ref, glng_ref, glnb_ref, ws_ref, bst_ref, wout_ref,
               lng_ref, lnb_ref, o_ref, v_ref, m_ref):
    tm = x_ref.shape[0]
    half = wout_ref.shape[0]
    gdim = half // GMLP_GROUPS
    x = x_ref[...]
    h = _modulate(x, mod_ref, 1).astype(BF16)

    def project(lo):
        return jnp.dot(h, win_ref[:, lo:lo + gdim], preferred_element_type=F32) + bin_ref[:, lo:lo + gdim]

    order = [half + g * gdim for g in range(GMLP_GROUPS)] + [g * gdim for g in range(GMLP_GROUPS)]
    pending = [project(lo) for lo in order[:GMLP_DOTS_AHEAD]]

    def next_projection(n):
        z = pending.pop(0)
        if n + GMLP_DOTS_AHEAD < len(order):
            pending.append(project(order[n + GMLP_DOTS_AHEAD]))
        return z

    for g in range(GMLP_GROUPS):
        v_ref[:, g * gdim:(g + 1) * gdim] = _gelu(next_projection(g))
    v = v_ref[...]
    mu = jnp.mean(v, axis=-1, keepdims=True)
    dv = v - mu
    rstd = lax.rsqrt(jnp.mean(dv * dv, axis=-1, keepdims=True) + LN_EPS)

    t_out = lax.broadcasted_iota(jnp.int32, (GMLP_WINDOW, GMLP_WINDOW), 0)
    s_in = lax.broadcasted_iota(jnp.int32, (GMLP_WINDOW, GMLP_WINDOW), 1)
    causal = (s_in // CHUNK) <= (t_out // CHUNK)

    for g in range(GMLP_GROUPS):
        lo = g * gdim
        vn = ((v_ref[:, lo:lo + gdim] - mu) * rstd * glng_ref[:, lo:lo + gdim]
              + glnb_ref[:, lo:lo + gdim]).astype(BF16)
        ws_g = jnp.where(causal, ws_ref[g], 0.0).astype(BF16)
        b_col = bst_ref[:, g:g + 1]
        u = _gelu(next_projection(GMLP_GROUPS + g))
        for w in range(tm // GMLP_WINDOW):
            rows = slice(w * GMLP_WINDOW, (w + 1) * GMLP_WINDOW)
            s = jnp.dot(ws_g, vn[rows], preferred_element_type=F32) + b_col
            m_ref[rows, lo:lo + gdim] = (u[rows] * s).astype(BF16)

    y = jnp.dot(m_ref[...], wout_ref[...], preferred_element_type=F32)
    r = ALPHA * x + _gate(mod_ref, 1) * y
    o_ref[...] = _layer_norm(r, lng_ref[1:2, :], lnb_ref[1:2, :])


def _gmlp(x, mod, w_in, b_in, gln_g, gln_b, w_s, b_s_t, w_out, ln_g, ln_b, *, layer, seq, tm):
    m, d = x.shape
    width = w_in.shape[2]
    half = width // 2
    tiles_per_seq = seq // tm
    const = lambda i: (layer, 0, 0)
    return pl.pallas_call(
        _gmlp_body,
        grid=(m // tm,),
        in_specs=[
            pl.BlockSpec((tm, d), lambda i: (i, 0)),
            pl.BlockSpec((None, None, N_MOD, d), lambda i: (layer, i // tiles_per_seq, 0, 0)),
            _resident((None, d, width), const),
            _resident((None, 1, width), const),
            _resident((None, 1, half), const),
            _resident((None, 1, half), const),
            _resident((None, GMLP_GROUPS, GMLP_WINDOW, GMLP_WINDOW), lambda i: (layer, 0, 0, 0)),
            _resident((None, GMLP_WINDOW, GMLP_GROUPS), const),
            _resident((None, half, d), const),
            _resident((None, 3, d), const),
            _resident((None, 3, d), const),
        ],
        out_specs=pl.BlockSpec((tm, d), lambda i: (i, 0)),
        out_shape=jax.ShapeDtypeStruct((m, d), F32),
        scratch_shapes=[pltpu.VMEM((tm, half), F32), pltpu.VMEM((tm, half), BF16)],
        compiler_params=_params(1),
        name="gmlp",
    )(x, mod, w_in, b_in, gln_g, gln_b, w_s, b_s_t, w_out, ln_g, ln_b)


def _kv_body(x_ref, mkv_ref, wk_ref, wvt_ref, k_ref, vt_ref):
    i = pl.program_id(1)

    @pl.when(i == 0)
    def _():
        k_ref[...] = jnp.zeros_like(k_ref)
        vt_ref[...] = jnp.zeros_like(vt_ref)

    @pl.when(i > 0)
    def _():
        h = (x_ref[...] * (1.0 + mkv_ref[1:2, :]) + mkv_ref[0:1, :]).astype(BF16)
        k = jnp.dot(h, wk_ref[...], preferred_element_type=F32)
        k_ref[...] = k.astype(BF16).reshape(k_ref.shape)
        vt = lax.dot_general(wvt_ref[...], h, (((1,), (1,)), ((), ())), preferred_element_type=F32)
        for n in range(vt_ref.shape[0]):
            vt_ref[n] = vt[:, n * KV_BLOCK:(n + 1) * KV_BLOCK].astype(BF16)


def _kv_project(x, mkv, w_k, w_v_t):
    bsz, seq, d = x.shape
    tm = LEFT_PAD
    blocks_per_tile = tm // KV_BLOCK
    n_blocks = (LEFT_PAD + seq) // KV_BLOCK
    return pl.pallas_call(
        _kv_body,
        grid=(bsz, 1 + seq // tm),
        in_specs=[
            pl.BlockSpec((None, tm, d), lambda b, i: (b, jnp.maximum(i - 1, 0), 0)),
            pl.BlockSpec((None, 2, d), lambda b, i: (b, 0, 0)),
            _resident((d, d), lambda b, i: (0, 0)),
            _resident((d, d), lambda b, i: (0, 0)),
        ],
        out_specs=[
            pl.BlockSpec((None, blocks_per_tile, KV_BLOCK, d), lambda b, i: (b, i, 0, 0)),
            pl.BlockSpec((None, blocks_per_tile, d, KV_BLOCK), lambda b, i: (b, i, 0, 0)),
        ],
        out_shape=[
            jax.ShapeDtypeStruct((bsz, n_blocks, KV_BLOCK, d), BF16),
            jax.ShapeDtypeStruct((bsz, n_blocks, d, KV_BLOCK), BF16),
        ],
        compiler_params=_params(2),
        name="kv_project",
    )(x, mkv, w_k, w_v_t)


def _bias_body(rb_ref, o_ref):
    rows, tq = o_ref.shape[1], o_ref.shape[2]
    width = 2 * tq
    r0 = pl.program_id(0) * rows
    rb = rb_ref[...]
    hi = rb.astype(BF16)
    rem = rb - hi.astype(F32)
    mid = rem.astype(BF16)
    lo = (rem - mid.astype(F32)).astype(BF16)
    lane = lax.broadcasted_iota(jnp.int32, (1, width), 1)
    rel = lax.broadcasted_iota(jnp.int32, (N_REL, width), 0)
    idx = jnp.clip(lane - (rows - 1) - r0 + LEFT_PAD, -(CHUNK - 1), MAX_REL) + (CHUNK - 1)
    onehot = (rel == idx).astype(BF16)
    wide = (jnp.dot(hi, onehot, preferred_element_type=F32)
            + jnp.dot(mid, onehot, preferred_element_type=F32)
            + jnp.dot(lo, onehot, preferred_element_type=F32)) * LOG2E
    qc = lax.broadcasted_iota(jnp.int32, (1, tq), 1) // CHUNK
    for k in range(rows):
        kc = (r0 + k) // CHUNK
        visible = (kc >= qc) & (kc <= qc + LEFT_CHUNKS)
        o_ref[:, k, :] = jnp.where(visible, wide[:, rows - 1 - k:rows - 1 - k + tq], NEG_BIG)


def _bias_table(rel_bias, tq):
    nh = rel_bias.shape[0]
    tk = tq + LEFT_PAD
    rows = BIAS_ROWS_PER_STEP
    assert tq + rows - 1 <= 2 * tq and tk % rows == 0
    return pl.pallas_call(
        _bias_body,
        grid=(tk // rows,),
        in_specs=[pl.BlockSpec((nh, N_REL), lambda i: (0, 0))],
        out_specs=pl.BlockSpec((nh, rows, tq), lambda i: (0, i, 0)),
        out_shape=jax.ShapeDtypeStruct((nh, tk, tq), F32),
        compiler_params=_params(1),
        name="bias_table",
    )(rel_bias)


def _attn_step(first_tile, x_ref, mod_ref, wq_ref, wo_ref, k_ref, vt_ref, bias_ref, lng_ref, lnb_ref,
               o_ref, ctxt_ref):
    tq, d = x_ref.shape
    n_sub = tq // KV_BLOCK
    win_blocks = (KV_BLOCK + LEFT_PAD) // KV_BLOCK
    tk = win_blocks * KV_BLOCK
    sub_k = ATTN_SUB_Q + LEFT_PAD
    hd = d // N_HEADS
    first_block = 0 if first_tile else pl.program_id(1) * n_sub
    x = x_ref[...]
    h = _modulate(x, mod_ref, 1)
    q = jnp.dot(h.astype(BF16), wq_ref[...], preferred_element_type=F32) * (hd ** -0.5 * LOG2E)
    q = q.astype(BF16)

    first_head = lax.broadcasted_iota(jnp.int32, (KV_BLOCK, V7X_LANES), 1) < hd
    ones_rows = jnp.ones((V7X_BF16_SUBLANES, KV_BLOCK), BF16)

    def padding_blocks(sub):
        return max(LEFT_PAD - sub * KV_BLOCK, 0) // KV_BLOCK if first_tile else 0

    def scores_t(sub, head):
        lanes = slice(head // 2 * V7X_LANES, (head // 2 + 1) * V7X_LANES)
        q_pair = q[sub * KV_BLOCK:(sub + 1) * KV_BLOCK, lanes]
        skip = padding_blocks(sub)
        k_win = k_ref[pl.ds(first_block + sub + skip, win_blocks - skip), :, lanes]
        k_win = k_win.reshape((win_blocks - skip) * KV_BLOCK, V7X_LANES)
        q_one = jnp.where(first_head if head % 2 == 0 else ~first_head, q_pair, jnp.zeros_like(q_pair))
        return lax.dot_general(k_win, q_one, (((1,), (1,)), ((), ())), preferred_element_type=F32)

    def softmax_t(sub, head, s_t):
        first_row = padding_blocks(sub) * KV_BLOCK
        probs = []
        for r0 in range(0, KV_BLOCK, ATTN_SUB_Q):
            lo = max(r0, first_row)
            s_sub = s_t[lo - first_row:r0 + sub_k - first_row, r0:r0 + ATTN_SUB_Q] + bias_ref[head, lo - r0:, :]
            s_max = jnp.max(s_sub, axis=0, keepdims=True)
            pieces = [jnp.exp2(s_sub - s_max).astype(BF16)]
            if lo - first_row:
                pieces.insert(0, jnp.zeros((lo - first_row, ATTN_SUB_Q), BF16))
            if tk - sub_k - r0:
                pieces.append(jnp.zeros((tk - sub_k - r0, ATTN_SUB_Q), BF16))
            probs.append(jnp.concatenate(pieces, axis=0))
        return jnp.concatenate(probs, axis=1)

    def context_t(sub, head, p_t):
        rows = slice(head * hd, (head + 1) * hd)
        skip = padding_blocks(sub)
        acc = None
        for n in range(skip, win_blocks):
            v_t = vt_ref[pl.ds(first_block + sub + n, 1), rows, :].reshape(hd, KV_BLOCK)
            lhs = jnp.concatenate([v_t, ones_rows], axis=0)
            part = jnp.dot(lhs, p_t[(n - skip) * KV_BLOCK:(n - skip + 1) * KV_BLOCK], preferred_element_type=F32)
            acc = part if acc is None else acc + part
        ctxt_ref[rows, sub * KV_BLOCK:(sub + 1) * KV_BLOCK] = (acc[:hd] / acc[hd:hd + 1]).astype(BF16)

    items = [(sub, head) for sub in range(n_sub) for head in range(N_HEADS)]
    pending = [scores_t(*item) for item in items[:ATTN_HEADS_AHEAD]]
    for n, item in enumerate(items):
        s_cur = pending.pop(0)
        if n + ATTN_HEADS_AHEAD < len(items):
            pending.append(scores_t(*items[n + ATTN_HEADS_AHEAD]))
        context_t(*item, softmax_t(*item, s_cur))

    y = lax.dot_general(ctxt_ref[...], wo_ref[...], (((0,), (0,)), ((), ())), preferred_element_type=F32)
    r = ALPHA * x + _gate(mod_ref, 1) * y
    o_ref[...] = _layer_norm(r, lng_ref[1:2, :], lnb_ref[1:2, :])


def _attn_body(x_ref, *refs):
    first_tile = pl.program_id(1) == 0
    pl.when(first_tile)(functools.partial(_attn_step, True, x_ref, *refs))
    pl.when(jnp.logical_not(first_tile))(functools.partial(_attn_step, False, x_ref, *refs))


def _attention(x, mod, w_q, w_o, k_pad, vt_pad, bias, ln_g, ln_b, *, layer, j):
    bsz, seq, d = x.shape
    tq = ATTN_TQ
    assert tq % KV_BLOCK == 0 and tq >= LEFT_PAD
    kv_blocks = k_pad.shape[1]
    return pl.pallas_call(
        _attn_body,
        grid=(bsz, seq // tq),
        in_specs=[
            pl.BlockSpec((None, tq, d), lambda b, i: (b, i, 0)),
            pl.BlockSpec((None, None, N_MOD, d), lambda b, i: (layer, b, 0, 0)),
            _resident((None, d, d), lambda b, i: (j, 0, 0)),
            _resident((None, d, d), lambda b, i: (j, 0, 0)),
            pl.BlockSpec((None, kv_blocks, KV_BLOCK, d), lambda b, i: (b, 0, 0, 0)),
            pl.BlockSpec((None, kv_blocks, d, KV_BLOCK), lambda b, i: (b, 0, 0, 0)),
            _resident((N_HEADS, ATTN_SUB_Q + LEFT_PAD, ATTN_SUB_Q), lambda b, i: (j, 0, 0)),
            _resident((None, 3, d), lambda b, i: (layer, 0, 0)),
            _resident((None, 3, d), lambda b, i: (layer, 0, 0)),
        ],
        out_specs=pl.BlockSpec((None, tq, d), lambda b, i: (b, i, 0)),
        out_shape=jax.ShapeDtypeStruct((bsz, seq, d), F32),
        scratch_shapes=[pltpu.VMEM((d, tq), BF16)],
        compiler_params=_params(2),
        name="attention",
    )(x, mod, w_q, w_o, k_pad, vt_pad, bias, ln_g, ln_b)


def kernel(x, c, w_ada, b_ada, ln_g, ln_b, ffn_gu, ffn_down, gmlp_w_in, gmlp_b_in, gmlp_ln_g, gmlp_ln_b,
           gmlp_w_s, gmlp_b_s, gmlp_w_out, w_ada_kv, b_ada_kv, w_kv, attn_w_q, attn_rel_bias, attn_w_o):
    bsz, seq, d = x.shape
    mod = _adaln(c, w_ada, b_ada, ADALN_TN).reshape(DEPTH, bsz, N_MOD, d)
    mkv = _adaln(c, w_ada_kv[None], b_ada_kv[None], w_ada_kv.shape[1]).reshape(bsz, 2, d)

    w_in_b = gmlp_w_in.astype(BF16)
    w_out_b = gmlp_w_out.astype(BF16)
    w_k_b = w_kv[:, :d].astype(BF16)
    w_v_t_b = w_kv[:, d:].T.astype(BF16)
    w_q_b = attn_w_q.astype(BF16)
    w_o_b = attn_w_o.astype(BF16)
    b_in = gmlp_b_in[:, None, :]
    gln_g = gmlp_ln_g[:, None, :]
    gln_b = gmlp_ln_b[:, None, :]
    b_s_t = jnp.swapaxes(gmlp_b_s, 1, 2)
    bias = _bias_table(attn_rel_bias.reshape(N_B * N_HEADS, N_REL), ATTN_SUB_Q)

    xf = x.reshape(bsz * seq, d)
    k_pad = vt_pad = None
    for l in range(DEPTH):
        ffn = functools.partial(_ffn, mod=mod, w_gu=ffn_gu, w_down=ffn_down, ln_g=ln_g, ln_b=ln_b,
                                layer=l, seq=seq, tm=FFN_TM)
        xf = ffn(xf, sub=0, half=0)
        if l < N_A:
            xf = _gmlp(xf, mod, w_in_b, b_in, gln_g, gln_b, gmlp_w_s, b_s_t, w_out_b, ln_g, ln_b,
                       layer=l, seq=seq, tm=GMLP_TM)
        else:
            xf = _attention(xf.reshape(bsz, seq, d), mod, w_q_b, w_o_b, k_pad, vt_pad, bias, ln_g, ln_b,
                            layer=l, j=l - N_A).reshape(bsz * seq, d)
        xf = ffn(xf, sub=2, half=1)
        if l == N_A - 1:
            k_pad, vt_pad = _kv_project(xf.reshape(bsz, seq, d), mkv, w_k_b, w_v_t_b)
    return xf.reshape(bsz, seq, d)
```

```python
import functools

import jax
import jax.numpy as jnp
from jax import lax
from jax.experimental import pallas as pl
from jax.experimental.pallas import tpu as pltpu

DEPTH = 4
CHUNK = 64
N_A = DEPTH // 2
N_B = DEPTH - N_A
GMLP_WINDOW = 128
GMLP_GROUPS = 8
N_HEADS = 16
LEFT_CHUNKS = 8
LEFT_PAD = LEFT_CHUNKS * CHUNK
MAX_REL = 4 * CHUNK
N_REL = (CHUNK - 1) + MAX_REL + 1
ALPHA = (2.0 * DEPTH) ** 0.25
LN_EPS = 1e-5
N_MOD = 9

V7X_LANES = 128
V7X_MXU_N = 256
V7X_BF16_SUBLANES = 16
V7X_VMEM_LIMIT_BYTES = 56 * 1024 * 1024

NEG_BIG = -1e30
LOG2E = 1.4426950408889634

KV_BLOCK = 256
ATTN_TQ = 512
ATTN_SUB_Q = 128
ATTN_HEADS_AHEAD = 2

FFN_TM = 512
FFN_FINISH_BLOCKS = 8
FFN_FIRST_FINISH_CHUNK = 1
FFN_WEIGHT_SLOTS = 3
GMLP_TM = 512
KV_TM = 1024
GMLP_DOTS_AHEAD = 2
ADALN_TN = 3072
BIAS_ROWS_PER_STEP = 64

F32 = jnp.float32
BF16 = jnp.bfloat16


def _params(n_axes):
    return pltpu.CompilerParams(
        dimension_semantics=("arbitrary",) * n_axes,
        vmem_limit_bytes=V7X_VMEM_LIMIT_BYTES,
    )


def _resident(block_shape, index_map):
    return pl.BlockSpec(block_shape, index_map, pipeline_mode=pl.Buffered(1))


def _layer_norm(r, g, b):
    mu = jnp.mean(r, axis=-1, keepdims=True)
    d = r - mu
    var = jnp.mean(d * d, axis=-1, keepdims=True)
    return d * lax.rsqrt(var + LN_EPS) * g + b


def _gelu(z):
    return 0.5 * z * (1.0 + lax.erf(z * (0.5 ** 0.5)))


def _modulate(x, mod_ref, sub):
    shift = mod_ref[3 * sub:3 * sub + 1, :]
    scale = mod_ref[3 * sub + 1:3 * sub + 2, :]
    return x * (1.0 + scale) + shift


def _gate(mod_ref, sub):
    return 1.0 + mod_ref[3 * sub + 2:3 * sub + 3, :]


def _zero_bits_of(v):
    bits = lax.bitcast_convert_type(v, jnp.int32)
    sixteen = jnp.full(bits.shape, 16, jnp.int32)
    cleared = lax.shift_right_logical(lax.shift_right_logical(bits, sixteen), sixteen)
    return jnp.max(cleared, axis=0, keepdims=True)


def _after(v, finished_rows):
    zero = _zero_bits_of(finished_rows)[:, :v.shape[1]]
    return jnp.where(zero == 0, v, jnp.zeros_like(v))


def _finish_rows(sub, n_blocks, lng_ref, lnb_ref, r_ref, o_ref, k):
    n = o_ref.shape[0] // n_blocks
    rows = slice(k * n, (k + 1) * n)
    out = _layer_norm(r_ref[rows, :], lng_ref[sub:sub + 1, :], lnb_ref[sub:sub + 1, :])
    o_ref[rows, :] = out
    return out


def _pipelined_specs(n_tiles, tiles_per_seq, tm, d, layer):
    cur = lambda i: jnp.minimum(i, n_tiles - 1)
    prev = lambda i: jnp.maximum(i - 1, 0)
    in_specs = [
        pl.BlockSpec((tm, d), lambda i: (cur(i), 0)),
        pl.BlockSpec((None, None, N_MOD, d), lambda i: (layer, cur(i) // tiles_per_seq, 0, 0)),
    ]
    out_spec = pl.BlockSpec((tm, d), lambda i: (prev(i), 0))
    parking = [pltpu.VMEM((tm, d), F32)]
    return in_specs, out_spec, parking


def _adaln_body(c_ref, w_ref, b_ref, o_ref):
    c = c_ref[...]
    c_act = c * jax.nn.sigmoid(c)
    o_ref[...] = jnp.dot(c_act, w_ref[...], preferred_element_type=F32) + b_ref[...]


def _adaln(c, w, b, tn):
    n_l, d, n = w.shape
    bsz = c.shape[0]
    return pl.pallas_call(
        _adaln_body,
        grid=(n_l, n // tn),
        in_specs=[
            pl.BlockSpec((bsz, d), lambda l, j: (0, 0)),
            pl.BlockSpec((None, d, tn), lambda l, j: (l, 0, j)),
            pl.BlockSpec((None, 1, tn), lambda l, j: (l, 0, j)),
        ],
        out_specs=pl.BlockSpec((None, bsz, tn), lambda l, j: (l, 0, j)),
        out_shape=jax.ShapeDtypeStruct((n_l, bsz, n), F32),
        compiler_params=_params(2),
        name="adaln",
    )(c, w, b.reshape(n_l, 1, n))


def _ffn_weight_copies(layer, half, d_ff, wgu_hbm, wd_hbm, stage_gu_ref, stage_d_ref, sem_ref, j):
    lo = j * V7X_MXU_N
    slot = j % FFN_WEIGHT_SLOTS
    return (
        pltpu.make_async_copy(wgu_hbm.at[layer, half, :, pl.ds(lo, V7X_MXU_N)],
                              stage_gu_ref.at[slot, 0], sem_ref.at[slot, 0]),
        pltpu.make_async_copy(wgu_hbm.at[layer, half, :, pl.ds(d_ff + lo, V7X_MXU_N)],
                              stage_gu_ref.at[slot, 1], sem_ref.at[slot, 1]),
        pltpu.make_async_copy(wd_hbm.at[layer, half, pl.ds(lo, V7X_MXU_N), :],
                              stage_d_ref.at[slot], sem_ref.at[slot, 2]),
    )


def _ffn_take_chunk(copies, n_chunks, d_ff, wgu_ref, wd_ref, stage_gu_ref, stage_d_ref, j):
    for copy in copies(j):
        copy.wait()
    lo = j * V7X_MXU_N
    slot = j % FFN_WEIGHT_SLOTS
    wgu_ref[:, lo:lo + V7X_MXU_N] = stage_gu_ref[slot, 0].astype(BF16)
    wgu_ref[:, d_ff + lo:d_ff + lo + V7X_MXU_N] = stage_gu_ref[slot, 1].astype(BF16)
    wd_ref[lo:lo + V7X_MXU_N, :] = stage_d_ref[slot].astype(BF16)
    if j + FFN_WEIGHT_SLOTS < n_chunks:
        for copy in copies(j + FFN_WEIGHT_SLOTS):
            copy.start()


def _ffn_start(sub, d_ff, x_ref, mod_ref, wgu_ref, wd_ref, a_ref, r_ref, finish_rows=None, take_chunk=None):
    x = x_ref[...]
    h = _modulate(x, mod_ref, sub).astype(BF16)
    for j in range(d_ff // V7X_MXU_N):
        lo = j * V7X_MXU_N
        if take_chunk is not None:
            take_chunk(j)
        g = jnp.dot(h, wgu_ref[:, lo:lo + V7X_MXU_N], preferred_element_type=F32)
        u = jnp.dot(h, wgu_ref[:, d_ff + lo:d_ff + lo + V7X_MXU_N], preferred_element_type=F32)
        act = g * jax.nn.sigmoid(g) * u
        k = j - FFN_FIRST_FINISH_CHUNK
        if finish_rows is not None and 0 <= k < FFN_FINISH_BLOCKS:
            act = _after(act, finish_rows(k))
        a_ref[:, lo:lo + V7X_MXU_N] = act.astype(BF16)
    y = jnp.dot(a_ref[...], wd_ref[...], preferred_element_type=F32)
    r_ref[...] = ALPHA * x + (0.5 * _gate(mod_ref, sub)) * y


def _ffn_body(layer, half, sub, d_ff, x_ref, mod_ref, wgu_hbm, wd_hbm, lng_ref, lnb_ref, o_ref,
              wgu_ref, wd_ref, stage_gu_ref, stage_d_ref, sem_ref, a_ref, r_ref):
    i = pl.program_id(0)
    last = pl.num_programs(0) - 1
    n_chunks = d_ff // V7X_MXU_N
    start = functools.partial(_ffn_start, sub, d_ff, x_ref, mod_ref, wgu_ref, wd_ref, a_ref, r_ref)
    finish_rows = functools.partial(_finish_rows, sub, FFN_FINISH_BLOCKS, lng_ref, lnb_ref, r_ref, o_ref)
    copies = functools.partial(_ffn_weight_copies, layer, half, d_ff, wgu_hbm, wd_hbm, stage_gu_ref, stage_d_ref,
                               sem_ref)
    take_chunk = functools.partial(_ffn_take_chunk, copies, n_chunks, d_ff, wgu_ref, wd_ref, stage_gu_ref,
                                   stage_d_ref)

    @pl.when(i == 0)
    def _():
        for j in range(FFN_WEIGHT_SLOTS):
            for copy in copies(j):
                copy.start()
        start(take_chunk=take_chunk)

    @pl.when((i > 0) & (i < last))
    def _():
        start(finish_rows=finish_rows)

    @pl.when(i == last)
    def _():
        for k in range(FFN_FINISH_BLOCKS):
            finish_rows(k)


def _ffn(x, mod, w_gu, w_down, ln_g, ln_b, *, layer, sub, half, seq, tm):
    m, d = x.shape
    d_ff = w_down.shape[2]
    n_tiles = m // tm
    in_specs, out_spec, parking = _pipelined_specs(n_tiles, seq // tm, tm, d, layer)
    return pl.pallas_call(
        functools.partial(_ffn_body, layer, half, sub, d_ff),
        grid=(n_tiles + 1,),
        in_specs=in_specs + [
            pl.BlockSpec(memory_space=pl.ANY),
            pl.BlockSpec(memory_space=pl.ANY),
            _resident((None, 3, d), lambda i: (layer, 0, 0)),
            _resident((None, 3, d), lambda i: (layer, 0, 0)),
        ],
        out_specs=out_spec,
        out_shape=jax.ShapeDtypeStruct((m, d), F32),
        scratch_shapes=[
            pltpu.VMEM((d, 2 * d_ff), BF16),
            pltpu.VMEM((d_ff, d), BF16),
            pltpu.VMEM((FFN_WEIGHT_SLOTS, 2, d, V7X_MXU_N), F32),
            pltpu.VMEM((FFN_WEIGHT_SLOTS, V7X_MXU_N, d), F32),
            pltpu.SemaphoreType.DMA((FFN_WEIGHT_SLOTS, 3)),
            pltpu.VMEM((tm, d_ff), BF16),
        ] + parking,
        compiler_params=_params(1),
        name="ffn",
    )(x, mod, w_gu, w_down, ln_g, ln_b)


def _gmlp_body(x_ref, mod_ref, win_ref, bin_ref, glng_ref, glnb_ref, ws_ref, bst_ref, wout_ref,
               lng_ref, lnb_ref, o_ref, v_ref, m_ref):
    tm = x_ref.shape[0]
    half = wout_ref.shape[0]
    gdim = half // GMLP_GROUPS
    x = x_ref[...]
    h = _modulate(x, mod_ref, 1).astype(BF16)

    def project(lo):
        return jnp.dot(h, win_ref[:, lo:lo + gdim], preferred_element_type=F32) + bin_ref[:, lo:lo + gdim]

    order = [half + g * gdim for g in range(GMLP_GROUPS)] + [g * gdim for g in range(GMLP_GROUPS)]
    pending = [project(lo) for lo in order[:GMLP_DOTS_AHEAD]]

    def next_projection(n):
        z = pending.pop(0)
        if n + GMLP_DOTS_AHEAD < len(order):
            pending.append(project(order[n + GMLP_DOTS_AHEAD]))
        return z

    for g in range(GMLP_GROUPS):
        v_ref[:, g * gdim:(g + 1) * gdim] = _gelu(next_projection(g))
    v = v_ref[...]
    mu = jnp.mean(v, axis=-1, keepdims=True)
    dv = v - mu
    rstd = lax.rsqrt(jnp.mean(dv * dv, axis=-1, keepdims=True) + LN_EPS)

    t_out = lax.broadcasted_iota(jnp.int32, (GMLP_WINDOW, GMLP_WINDOW), 0)
    s_in = lax.broadcasted_iota(jnp.int32, (GMLP_WINDOW, GMLP_WINDOW), 1)
    causal = (s_in // CHUNK) <= (t_out // CHUNK)

    for g in range(GMLP_GROUPS):
        lo = g * gdim
        vn = ((v_ref[:, lo:lo + gdim] - mu) * rstd * glng_ref[:, lo:lo + gdim]
              + glnb_ref[:, lo:lo + gdim]).astype(BF16)
        ws_g = jnp.where(causal, ws_ref[g], 0.0).astype(BF16)
        b_col = bst_ref[:, g:g + 1]
        u = _gelu(next_projection(GMLP_GROUPS + g))
        for w in range(tm // GMLP_WINDOW):
            rows = slice(w * GMLP_WINDOW, (w + 1) * GMLP_WINDOW)
            s = jnp.dot(ws_g, vn[rows], preferred_element_type=F32) + b_col
            m_ref[rows, lo:lo + gdim] = (u[rows] * s).astype(BF16)

    y = jnp.dot(m_ref[...], wout_ref[...], preferred_element_type=F32)
    r = ALPHA * x + _gate(mod_ref, 1) * y
    o_ref[...] = _layer_norm(r, lng_ref[1:2, :], lnb_ref[1:2, :])


def _gmlp(x, mod, w_in, b_in, gln_g, gln_b, w_s, b_s_t, w_out, ln_g, ln_b, *, layer, seq, tm):
    m, d = x.shape
    width = w_in.shape[2]
    half = width // 2
    tiles_per_seq = seq // tm
    const = lambda i: (layer, 0, 0)
    return pl.pallas_call(
        _gmlp_body,
        grid=(m // tm,),
        in_specs=[
            pl.BlockSpec((tm, d), lambda i: (i, 0)),
            pl.BlockSpec((None, None, N_MOD, d), lambda i: (layer, i // tiles_per_seq, 0, 0)),
            _resident((None, d, width), const),
            _resident((None, 1, width), const),
            _resident((None, 1, half), const),
            _resident((None, 1, half), const),
            _resident((None, GMLP_GROUPS, GMLP_WINDOW, GMLP_WINDOW), lambda i: (layer, 0, 0, 0)),
            _resident((None, GMLP_WINDOW, GMLP_GROUPS), const),
            _resident((None, half, d), const),
            _resident((None, 3, d), const),
            _resident((None, 3, d), const),
        ],
        out_specs=pl.BlockSpec((tm, d), lambda i: (i, 0)),
        out_shape=jax.ShapeDtypeStruct((m, d), F32),
        scratch_shapes=[pltpu.VMEM((tm, half), F32), pltpu.VMEM((tm, half), BF16)],
        compiler_params=_params(1),
        name="gmlp",
    )(x, mod, w_in, b_in, gln_g, gln_b, w_s, b_s_t, w_out, ln_g, ln_b)


def _kv_body(x_ref, mkv_ref, wk_ref, wvt_ref, k_ref, vt_ref):
    h = (x_ref[...] * (1.0 + mkv_ref[1:2, :]) + mkv_ref[0:1, :]).astype(BF16)
    k = jnp.dot(h, wk_ref[...], preferred_element_type=F32)
    k_ref[...] = k.astype(BF16).reshape(k_ref.shape)
    vt = lax.dot_general(wvt_ref[...], h, (((1,), (1,)), ((), ())), preferred_element_type=F32)
    for n in range(vt_ref.shape[0]):
        vt_ref[n] = vt[:, n * KV_BLOCK:(n + 1) * KV_BLOCK].astype(BF16)


def _kv_project(x, mkv, w_k, w_v_t, tm):
    bsz, seq, d = x.shape
    blocks_per_tile = tm // KV_BLOCK
    n_blocks = seq // KV_BLOCK
    return pl.pallas_call(
        _kv_body,
        grid=(bsz, seq // tm),
        in_specs=[
            pl.BlockSpec((None, tm, d), lambda b, i: (b, i, 0)),
            pl.BlockSpec((None, 2, d), lambda b, i: (b, 0, 0)),
            _resident((d, d), lambda b, i: (0, 0)),
            _resident((d, d), lambda b, i: (0, 0)),
        ],
        out_specs=[
            pl.BlockSpec((None, blocks_per_tile, KV_BLOCK, d), lambda b, i: (b, i, 0, 0)),
            pl.BlockSpec((None, blocks_per_tile, d, KV_BLOCK), lambda b, i: (b, i, 0, 0)),
        ],
        out_shape=[
            jax.ShapeDtypeStruct((bsz, n_blocks, KV_BLOCK, d), BF16),
            jax.ShapeDtypeStruct((bsz, n_blocks, d, KV_BLOCK), BF16),
        ],
        compiler_params=_params(2),
        name="kv_project",
    )(x, mkv, w_k, w_v_t)


def _bias_body(rb_ref, o_ref):
    rows, tq = o_ref.shape[1], o_ref.shape[2]
    width = 2 * tq
    r0 = pl.program_id(0) * rows
    rb = rb_ref[...]
    hi = rb.astype(BF16)
    rem = rb - hi.astype(F32)
    mid = rem.astype(BF16)
    lo = (rem - mid.astype(F32)).astype(BF16)
    lane = lax.broadcasted_iota(jnp.int32, (1, width), 1)
    rel = lax.broadcasted_iota(jnp.int32, (N_REL, width), 0)
    idx = jnp.clip(lane - (rows - 1) - r0 + LEFT_PAD, -(CHUNK - 1), MAX_REL) + (CHUNK - 1)
    onehot = (rel == idx).astype(BF16)
    wide = (jnp.dot(hi, onehot, preferred_element_type=F32)
            + jnp.dot(mid, onehot, preferred_element_type=F32)
            + jnp.dot(lo, onehot, preferred_element_type=F32)) * LOG2E
    qc = lax.broadcasted_iota(jnp.int32, (1, tq), 1) // CHUNK
    for k in range(rows):
        kc = (r0 + k) // CHUNK
        visible = (kc >= qc) & (kc <= qc + LEFT_CHUNKS)
        o_ref[:, k, :] = jnp.where(visible, wide[:, rows - 1 - k:rows - 1 - k + tq], NEG_BIG)


def _bias_table(rel_bias, tq):
    nh = rel_bias.shape[0]
    tk = tq + LEFT_PAD
    rows = BIAS_ROWS_PER_STEP
    assert tq + rows - 1 <= 2 * tq and tk % rows == 0
    return pl.pallas_call(
        _bias_body,
        grid=(tk // rows,),
        in_specs=[pl.BlockSpec((nh, N_REL), lambda i: (0, 0))],
        out_specs=pl.BlockSpec((nh, rows, tq), lambda i: (0, i, 0)),
        out_shape=jax.ShapeDtypeStruct((nh, tk, tq), F32),
        compiler_params=_params(1),
        name="bias_table",
    )(rel_bias)


def _attn_step(first_tile, x_ref, mod_ref, wq_ref, wo_ref, k_ref, vt_ref, bias_ref, lng_ref, lnb_ref,
               o_ref, ctxt_ref):
    tq, d = x_ref.shape
    n_sub = tq // KV_BLOCK
    win_blocks = (KV_BLOCK + LEFT_PAD) // KV_BLOCK
    tk = win_blocks * KV_BLOCK
    sub_k = ATTN_SUB_Q + LEFT_PAD
    hd = d // N_HEADS
    first_block = (0 if first_tile else pl.program_id(1) * n_sub) - LEFT_PAD // KV_BLOCK
    x = x_ref[...]
    h = _modulate(x, mod_ref, 1)
    q = jnp.dot(h.astype(BF16), wq_ref[...], preferred_element_type=F32) * (hd ** -0.5 * LOG2E)
    q = q.astype(BF16)

    first_head = lax.broadcasted_iota(jnp.int32, (KV_BLOCK, V7X_LANES), 1) < hd
    ones_rows = jnp.ones((V7X_BF16_SUBLANES, KV_BLOCK), BF16)

    def padding_blocks(sub):
        return max(LEFT_PAD - sub * KV_BLOCK, 0) // KV_BLOCK if first_tile else 0

    def scores_t(sub, head):
        lanes = slice(head // 2 * V7X_LANES, (head // 2 + 1) * V7X_LANES)
        q_pair = q[sub * KV_BLOCK:(sub + 1) * KV_BLOCK, lanes]
        skip = padding_blocks(sub)
        k_win = k_ref[pl.ds(first_block + sub + skip, win_blocks - skip), :, lanes]
        k_win = k_win.reshape((win_blocks - skip) * KV_BLOCK, V7X_LANES)
        q_one = jnp.where(first_head if head % 2 == 0 else ~first_head, q_pair, jnp.zeros_like(q_pair))
        return lax.dot_general(k_win, q_one, (((1,), (1,)), ((), ())), preferred_element_type=F32)

    def softmax_t(sub, head, s_t):
        first_row = padding_blocks(sub) * KV_BLOCK
        probs = []
        for r0 in range(0, KV_BLOCK, ATTN_SUB_Q):
            lo = max(r0, first_row)
            s_sub = s_t[lo - first_row:r0 + sub_k - first_row, r0:r0 + ATTN_SUB_Q] + bias_ref[head, lo - r0:, :]
            s_max = jnp.max(s_sub, axis=0, keepdims=True)
            pieces = [jnp.exp2(s_sub - s_max).astype(BF16)]
            if lo - first_row:
                pieces.insert(0, jnp.zeros((lo - first_row, ATTN_SUB_Q), BF16))
            if tk - sub_k - r0:
                pieces.append(jnp.zeros((tk - sub_k - r0, ATTN_SUB_Q), BF16))
            probs.append(jnp.concatenate(pieces, axis=0))
        return jnp.concatenate(probs, axis=1)

    def context_t(sub, head, p_t):
        rows = slice(head * hd, (head + 1) * hd)
        skip = padding_blocks(sub)
        acc = None
        for n in range(skip, win_blocks):
            v_t = vt_ref[pl.ds(first_block + sub + n, 1), rows, :].reshape(hd, KV_BLOCK)
            lhs = jnp.concatenate([v_t, ones_rows], axis=0)
            part = jnp.dot(lhs, p_t[(n - skip) * KV_BLOCK:(n - skip + 1) * KV_BLOCK], preferred_element_type=F32)
            acc = part if acc is None else acc + part
        ctxt_ref[rows, sub * KV_BLOCK:(sub + 1) * KV_BLOCK] = (acc[:hd] / acc[hd:hd + 1]).astype(BF16)

    items = [(sub, head) for sub in range(n_sub) for head in range(N_HEADS)]
    pending = [scores_t(*item) for item in items[:ATTN_HEADS_AHEAD]]
    for n, item in enumerate(items):
        s_cur = pending.pop(0)
        if n + ATTN_HEADS_AHEAD < len(items):
            pending.append(scores_t(*items[n + ATTN_HEADS_AHEAD]))
        context_t(*item, softmax_t(*item, s_cur))

    y = lax.dot_general(ctxt_ref[...], wo_ref[...], (((0,), (0,)), ((), ())), preferred_element_type=F32)
    r = ALPHA * x + _gate(mod_ref, 1) * y
    o_ref[...] = _layer_norm(r, lng_ref[1:2, :], lnb_ref[1:2, :])


def _attn_body(x_ref, *refs):
    first_tile = pl.program_id(1) == 0
    pl.when(first_tile)(functools.partial(_attn_step, True, x_ref, *refs))
    pl.when(jnp.logical_not(first_tile))(functools.partial(_attn_step, False, x_ref, *refs))


def _attention(x, mod, w_q, w_o, k_blocks, vt_blocks, bias, ln_g, ln_b, *, layer, j):
    bsz, seq, d = x.shape
    tq = ATTN_TQ
    assert tq % KV_BLOCK == 0 and tq >= LEFT_PAD
    kv_blocks = k_blocks.shape[1]
    return pl.pallas_call(
        _attn_body,
        grid=(bsz, seq // tq),
        in_specs=[
            pl.BlockSpec((None, tq, d), lambda b, i: (b, i, 0)),
            pl.BlockSpec((None, None, N_MOD, d), lambda b, i: (layer, b, 0, 0)),
            _resident((None, d, d), lambda b, i: (j, 0, 0)),
            _resident((None, d, d), lambda b, i: (j, 0, 0)),
            pl.BlockSpec((None, kv_blocks, KV_BLOCK, d), lambda b, i: (b, 0, 0, 0)),
            pl.BlockSpec((None, kv_blocks, d, KV_BLOCK), lambda b, i: (b, 0, 0, 0)),
            _resident((N_HEADS, ATTN_SUB_Q + LEFT_PAD, ATTN_SUB_Q), lambda b, i: (j, 0, 0)),
            _resident((None, 3, d), lambda b, i: (layer, 0, 0)),
            _resident((None, 3, d), lambda b, i: (layer, 0, 0)),
        ],
        out_specs=pl.BlockSpec((None, tq, d), lambda b, i: (b, i, 0)),
        out_shape=jax.ShapeDtypeStruct((bsz, seq, d), F32),
        scratch_shapes=[pltpu.VMEM((d, tq), BF16)],
        compiler_params=_params(2),
        name="attention",
    )(x, mod, w_q, w_o, k_blocks, vt_blocks, bias, ln_g, ln_b)


def kernel(x, c, w_ada, b_ada, ln_g, ln_b, ffn_gu, ffn_down, gmlp_w_in, gmlp_b_in, gmlp_ln_g, gmlp_ln_b,
           gmlp_w_s, gmlp_b_s, gmlp_w_out, w_ada_kv, b_ada_kv, w_kv, attn_w_q, attn_rel_bias, attn_w_o):
    bsz, seq, d = x.shape
    mod = _adaln(c, w_ada, b_ada, ADALN_TN).reshape(DEPTH, bsz, N_MOD, d)
    mkv = _adaln(c, w_ada_kv[None], b_ada_kv[None], w_ada_kv.shape[1]).reshape(bsz, 2, d)

    w_in_b = gmlp_w_in.astype(BF16)
    w_out_b = gmlp_w_out.astype(BF16)
    w_k_b = w_kv[:, :d].astype(BF16)
    w_v_t_b = w_kv[:, d:].T.astype(BF16)
    w_q_b = attn_w_q.astype(BF16)
    w_o_b = attn_w_o.astype(BF16)
    b_in = gmlp_b_in[:, None, :]
    gln_g = gmlp_ln_g[:, None, :]
    gln_b = gmlp_ln_b[:, None, :]
    b_s_t = jnp.swapaxes(gmlp_b_s, 1, 2)
    bias = _bias_table(attn_rel_bias.reshape(N_B * N_HEADS, N_REL), ATTN_SUB_Q)

    xf = x.reshape(bsz * seq, d)
    k_blocks = vt_blocks = None
    for l in range(DEPTH):
        ffn = functools.partial(_ffn, mod=mod, w_gu=ffn_gu, w_down=ffn_down, ln_g=ln_g, ln_b=ln_b,
                                layer=l, seq=seq, tm=FFN_TM)
        xf = ffn(xf, sub=0, half=0)
        if l < N_A:
            xf = _gmlp(xf, mod, w_in_b, b_in, gln_g, gln_b, gmlp_w_s, b_s_t, w_out_b, ln_g, ln_b,
                       layer=l, seq=seq, tm=GMLP_TM)
        else:
            xf = _attention(xf.reshape(bsz, seq, d), mod, w_q_b, w_o_b, k_blocks, vt_blocks, bias, ln_g, ln_b,
                            layer=l, j=l - N_A).reshape(bsz * seq, d)
        xf = ffn(xf, sub=2, half=1)
        if l == N_A - 1:
            k_blocks, vt_blocks = _kv_project(xf.reshape(bsz, seq, d), mkv, w_k_b, w_v_t_b, KV_TM)
    return xf.reshape(bsz, seq, d)
```

```python
import functools

import jax
import jax.numpy as jnp
from jax import lax
from jax.experimental import pallas as pl
from jax.experimental.pallas import tpu as pltpu

DEPTH = 4
CHUNK = 64
N_A = DEPTH // 2
N_B = DEPTH - N_A
GMLP_WINDOW = 128
GMLP_GROUPS = 8
N_HEADS = 16
LEFT_CHUNKS = 8
LEFT_PAD = LEFT_CHUNKS * CHUNK
MAX_REL = 4 * CHUNK
N_REL = (CHUNK - 1) + MAX_REL + 1
ALPHA = (2.0 * DEPTH) ** 0.25
LN_EPS = 1e-5
N_MOD = 9

V7X_LANES = 128
V7X_MXU_N = 256
V7X_BF16_SUBLANES = 16
V7X_VMEM_LIMIT_BYTES = 56 * 1024 * 1024

NEG_BIG = -1e30
LOG2E = 1.4426950408889634

KV_BLOCK = 256
ATTN_TQ = 1024
ATTN_SUB_Q = 128
ATTN_HEADS_AHEAD = 2

FFN_TM = 512
FFN_FINISH_BLOCKS = 8
FFN_FIRST_FINISH_CHUNK = 1
FFN_WEIGHT_SLOTS = 3
GMLP_TM = 512
KV_TM = 1024
GMLP_DOTS_AHEAD = 2
ADALN_TN = 3072
BIAS_ROWS_PER_STEP = 64

F32 = jnp.float32
BF16 = jnp.bfloat16


def _params(n_axes):
    return pltpu.CompilerParams(
        dimension_semantics=("arbitrary",) * n_axes,
        vmem_limit_bytes=V7X_VMEM_LIMIT_BYTES,
    )


def _resident(block_shape, index_map):
    return pl.BlockSpec(block_shape, index_map, pipeline_mode=pl.Buffered(1))


def _layer_norm(r, g, b):
    mu = jnp.mean(r, axis=-1, keepdims=True)
    d = r - mu
    var = jnp.mean(d * d, axis=-1, keepdims=True)
    return d * lax.rsqrt(var + LN_EPS) * g + b


def _gelu(z):
    return 0.5 * z * (1.0 + lax.erf(z * (0.5 ** 0.5)))


def _modulate(x, mod_ref, sub):
    shift = mod_ref[3 * sub:3 * sub + 1, :]
    scale = mod_ref[3 * sub + 1:3 * sub + 2, :]
    return x * (1.0 + scale) + shift


def _gate(mod_ref, sub):
    return 1.0 + mod_ref[3 * sub + 2:3 * sub + 3, :]


def _zero_bits_of(v):
    bits = lax.bitcast_convert_type(v, jnp.int32)
    sixteen = jnp.full(bits.shape, 16, jnp.int32)
    cleared = lax.shift_right_logical(lax.shift_right_logical(bits, sixteen), sixteen)
    return jnp.max(cleared, axis=0, keepdims=True)


def _after(v, finished_rows):
    zero = _zero_bits_of(finished_rows)[:, :v.shape[1]]
    return jnp.where(zero == 0, v, jnp.zeros_like(v))


def _finish_rows(sub, n_blocks, lng_ref, lnb_ref, r_ref, o_ref, k):
    n = o_ref.shape[0] // n_blocks
    rows = slice(k * n, (k + 1) * n)
    out = _layer_norm(r_ref[rows, :], lng_ref[sub:sub + 1, :], lnb_ref[sub:sub + 1, :])
    o_ref[rows, :] = out
    return out


def _pipelined_specs(n_tiles, tiles_per_seq, tm, d, layer):
    cur = lambda i: jnp.minimum(i, n_tiles - 1)
    prev = lambda i: jnp.maximum(i - 1, 0)
    in_specs = [
        pl.BlockSpec((tm, d), lambda i: (cur(i), 0)),
        pl.BlockSpec((None, None, N_MOD, d), lambda i: (layer, cur(i) // tiles_per_seq, 0, 0)),
    ]
    out_spec = pl.BlockSpec((tm, d), lambda i: (prev(i), 0))
    parking = [pltpu.VMEM((tm, d), F32)]
    return in_specs, out_spec, parking


def _adaln_body(c_ref, w_ref, b_ref, o_ref):
    c = c_ref[...]
    c_act = c * jax.nn.sigmoid(c)
    o_ref[...] = jnp.dot(c_act, w_ref[...], preferred_element_type=F32) + b_ref[...]


def _adaln(c, w, b, tn):
    n_l, d, n = w.shape
    bsz = c.shape[0]
    return pl.pallas_call(
        _adaln_body,
        grid=(n_l, n // tn),
        in_specs=[
            pl.BlockSpec((bsz, d), lambda l, j: (0, 0)),
            pl.BlockSpec((None, d, tn), lambda l, j: (l, 0, j)),
            pl.BlockSpec((None, 1, tn), lambda l, j: (l, 0, j)),
        ],
        out_specs=pl.BlockSpec((None, bsz, tn), lambda l, j: (l, 0, j)),
        out_shape=jax.ShapeDtypeStruct((n_l, bsz, n), F32),
        compiler_params=_params(2),
        name="adaln",
    )(c, w, b.reshape(n_l, 1, n))


def _ffn_weight_copies(layer, half, d_ff, wgu_hbm, wd_hbm, stage_gu_ref, stage_d_ref, sem_ref, j):
    lo = j * V7X_MXU_N
    slot = j % FFN_WEIGHT_SLOTS
    return (
        pltpu.make_async_copy(wgu_hbm.at[layer, half, :, pl.ds(lo, V7X_MXU_N)],
                              stage_gu_ref.at[slot, 0], sem_ref.at[slot, 0]),
        pltpu.make_async_copy(wgu_hbm.at[layer, half, :, pl.ds(d_ff + lo, V7X_MXU_N)],
                              stage_gu_ref.at[slot, 1], sem_ref.at[slot, 1]),
        pltpu.make_async_copy(wd_hbm.at[layer, half, pl.ds(lo, V7X_MXU_N), :],
                              stage_d_ref.at[slot], sem_ref.at[slot, 2]),
    )


def _ffn_take_chunk(copies, n_chunks, d_ff, wgu_ref, wd_ref, stage_gu_ref, stage_d_ref, j):
    for copy in copies(j):
        copy.wait()
    lo = j * V7X_MXU_N
    slot = j % FFN_WEIGHT_SLOTS
    wgu_ref[:, lo:lo + V7X_MXU_N] = stage_gu_ref[slot, 0].astype(BF16)
    wgu_ref[:, d_ff + lo:d_ff + lo + V7X_MXU_N] = stage_gu_ref[slot, 1].astype(BF16)
    wd_ref[lo:lo + V7X_MXU_N, :] = stage_d_ref[slot].astype(BF16)
    if j + FFN_WEIGHT_SLOTS < n_chunks:
        for copy in copies(j + FFN_WEIGHT_SLOTS):
            copy.start()


def _ffn_start(sub, d_ff, x_ref, mod_ref, wgu_ref, wd_ref, a_ref, r_ref, finish_rows=None, take_chunk=None):
    x = x_ref[...]
    h = _modulate(x, mod_ref, sub).astype(BF16)
    for j in range(d_ff // V7X_MXU_N):
        lo = j * V7X_MXU_N
        if take_chunk is not None:
            take_chunk(j)
        g = jnp.dot(h, wgu_ref[:, lo:lo + V7X_MXU_N], preferred_element_type=F32)
        u = jnp.dot(h, wgu_ref[:, d_ff + lo:d_ff + lo + V7X_MXU_N], preferred_element_type=F32)
        act = g * jax.nn.sigmoid(g) * u
        k = j - FFN_FIRST_FINISH_CHUNK
        if finish_rows is not None and 0 <= k < FFN_FINISH_BLOCKS:
            act = _after(act, finish_rows(k))
        a_ref[:, lo:lo + V7X_MXU_N] = act.astype(BF16)
    y = jnp.dot(a_ref[...], wd_ref[...], preferred_element_type=F32)
    r_ref[...] = ALPHA * x + (0.5 * _gate(mod_ref, sub)) * y


def _ffn_body(layer, half, sub, d_ff, x_ref, mod_ref, wgu_hbm, wd_hbm, lng_ref, lnb_ref, o_ref,
              wgu_ref, wd_ref, stage_gu_ref, stage_d_ref, sem_ref, a_ref, r_ref):
    i = pl.program_id(0)
    last = pl.num_programs(0) - 1
    n_chunks = d_ff // V7X_MXU_N
    start = functools.partial(_ffn_start, sub, d_ff, x_ref, mod_ref, wgu_ref, wd_ref, a_ref, r_ref)
    finish_rows = functools.partial(_finish_rows, sub, FFN_FINISH_BLOCKS, lng_ref, lnb_ref, r_ref, o_ref)
    copies = functools.partial(_ffn_weight_copies, layer, half, d_ff, wgu_hbm, wd_hbm, stage_gu_ref, stage_d_ref,
                               sem_ref)
    take_chunk = functools.partial(_ffn_take_chunk, copies, n_chunks, d_ff, wgu_ref, wd_ref, stage_gu_ref,
                                   stage_d_ref)

    @pl.when(i == 0)
    def _():
        for j in range(FFN_WEIGHT_SLOTS):
            for copy in copies(j):
                copy.start()
        start(take_chunk=take_chunk)

    @pl.when((i > 0) & (i < last))
    def _():
        start(finish_rows=finish_rows)

    @pl.when(i == last)
    def _():
        for k in range(FFN_FINISH_BLOCKS):
            finish_rows(k)


def _ffn(x, mod, w_gu, w_down, ln_g, ln_b, *, layer, sub, half, seq, tm):
    m, d = x.shape
    d_ff = w_down.shape[2]
    n_tiles = m // tm
    in_specs, out_spec, parking = _pipelined_specs(n_tiles, seq // tm, tm, d, layer)
    return pl.pallas_call(
        functools.partial(_ffn_body, layer, half, sub, d_ff),
        grid=(n_tiles + 1,),
        in_specs=in_specs + [
            pl.BlockSpec(memory_space=pl.ANY),
            pl.BlockSpec(memory_space=pl.ANY),
            _resident((None, 3, d), lambda i: (layer, 0, 0)),
            _resident((None, 3, d), lambda i: (layer, 0, 0)),
        ],
        out_specs=out_spec,
        out_shape=jax.ShapeDtypeStruct((m, d), F32),
        scratch_shapes=[
            pltpu.VMEM((d, 2 * d_ff), BF16),
            pltpu.VMEM((d_ff, d), BF16),
            pltpu.VMEM((FFN_WEIGHT_SLOTS, 2, d, V7X_MXU_N), F32),
            pltpu.VMEM((FFN_WEIGHT_SLOTS, V7X_MXU_N, d), F32),
            pltpu.SemaphoreType.DMA((FFN_WEIGHT_SLOTS, 3)),
            pltpu.VMEM((tm, d_ff), BF16),
        ] + parking,
        compiler_params=_params(1),
        name="ffn",
    )(x, mod, w_gu, w_down, ln_g, ln_b)


def _gmlp_body(x_ref, mod_ref, win_ref, bin_ref, glng_ref, glnb_ref, ws_ref, bst_ref, wout_ref,
               lng_ref, lnb_ref, o_ref, v_ref, m_ref):
    tm = x_ref.shape[0]
    half = wout_ref.shape[0]
    gdim = half // GMLP_GROUPS
    x = x_ref[...]
    h = _modulate(x, mod_ref, 1).astype(BF16)

    def project(lo):
        return jnp.dot(h, win_ref[:, lo:lo + gdim], preferred_element_type=F32) + bin_ref[:, lo:lo + gdim]

    order = [half + g * gdim for g in range(GMLP_GROUPS)] + [g * gdim for g in range(GMLP_GROUPS)]
    pending = [project(lo) for lo in order[:GMLP_DOTS_AHEAD]]

    def next_projection(n):
        z = pending.pop(0)
        if n + GMLP_DOTS_AHEAD < len(order):
            pending.append(project(order[n + GMLP_DOTS_AHEAD]))
        return z

    for g in range(GMLP_GROUPS):
        v_ref[:, g * gdim:(g + 1) * gdim] = _gelu(next_projection(g))
    v = v_ref[...]
    mu = jnp.mean(v, axis=-1, keepdims=True)
    dv = v - mu
    rstd = lax.rsqrt(jnp.mean(dv * dv, axis=-1, keepdims=True) + LN_EPS)

    t_out = lax.broadcasted_iota(jnp.int32, (GMLP_WINDOW, GMLP_WINDOW), 0)
    s_in = lax.broadcasted_iota(jnp.int32, (GMLP_WINDOW, GMLP_WINDOW), 1)
    causal = (s_in // CHUNK) <= (t_out // CHUNK)

    for g in range(GMLP_GROUPS):
        lo = g * gdim
        vn = ((v_ref[:, lo:lo + gdim] - mu) * rstd * glng_ref[:, lo:lo + gdim]
              + glnb_ref[:, lo:lo + gdim]).astype(BF16)
        ws_g = jnp.where(causal, ws_ref[g], 0.0).astype(BF16)
        b_col = bst_ref[:, g:g + 1]
        u = _gelu(next_projection(GMLP_GROUPS + g))
        for w in range(tm // GMLP_WINDOW):
            rows = slice(w * GMLP_WINDOW, (w + 1) * GMLP_WINDOW)
            s = jnp.dot(ws_g, vn[rows], preferred_element_type=F32) + b_col
            m_ref[rows, lo:lo + gdim] = (u[rows] * s).astype(BF16)

    y = jnp.dot(m_ref[...], wout_ref[...], preferred_element_type=F32)
    r = ALPHA * x + _gate(mod_ref, 1) * y
    o_ref[...] = _layer_norm(r, lng_ref[1:2, :], lnb_ref[1:2, :])


def _gmlp(x, mod, w_in, b_in, gln_g, gln_b, w_s, b_s_t, w_out, ln_g, ln_b, *, layer, seq, tm):
    m, d = x.shape
    width = w_in.shape[2]
    half = width // 2
    tiles_per_seq = seq // tm
    const = lambda i: (layer, 0, 0)
    return pl.pallas_call(
        _gmlp_body,
        grid=(m // tm,),
        in_specs=[
            pl.BlockSpec((tm, d), lambda i: (i, 0)),
            pl.BlockSpec((None, None, N_MOD, d), lambda i: (layer, i // tiles_per_seq, 0, 0)),
            _resident((None, d, width), const),
            _resident((None, 1, width), const),
            _resident((None, 1, half), const),
            _resident((None, 1, half), const),
            _resident((None, GMLP_GROUPS, GMLP_WINDOW, GMLP_WINDOW), lambda i: (layer, 0, 0, 0)),
            _resident((None, GMLP_WINDOW, GMLP_GROUPS), const),
            _resident((None, half, d), const),
            _resident((None, 3, d), const),
            _resident((None, 3, d), const),
        ],
        out_specs=pl.BlockSpec((tm, d), lambda i: (i, 0)),
        out_shape=jax.ShapeDtypeStruct((m, d), F32),
        scratch_shapes=[pltpu.VMEM((tm, half), F32), pltpu.VMEM((tm, half), BF16)],
        compiler_params=_params(1),
        name="gmlp",
    )(x, mod, w_in, b_in, gln_g, gln_b, w_s, b_s_t, w_out, ln_g, ln_b)


def _kv_body(x_ref, mkv_ref, wk_ref, wvt_ref, k_ref, vt_ref):
    h = (x_ref[...] * (1.0 + mkv_ref[1:2, :]) + mkv_ref[0:1, :]).astype(BF16)
    k = jnp.dot(h, wk_ref[...], preferred_element_type=F32)
    k_ref[...] = k.astype(BF16).reshape(k_ref.shape)
    vt = lax.dot_general(wvt_ref[...], h, (((1,), (1,)), ((), ())), preferred_element_type=F32)
    for n in range(vt_ref.shape[0]):
        vt_ref[n] = vt[:, n * KV_BLOCK:(n + 1) * KV_BLOCK].astype(BF16)


def _kv_project(x, mkv, w_k, w_v_t, tm):
    bsz, seq, d = x.shape
    blocks_per_tile = tm // KV_BLOCK
    n_blocks = seq // KV_BLOCK
    return pl.pallas_call(
        _kv_body,
        grid=(bsz, seq // tm),
        in_specs=[
            pl.BlockSpec((None, tm, d), lambda b, i: (b, i, 0)),
            pl.BlockSpec((None, 2, d), lambda b, i: (b, 0, 0)),
            _resident((d, d), lambda b, i: (0, 0)),
            _resident((d, d), lambda b, i: (0, 0)),
        ],
        out_specs=[
            pl.BlockSpec((None, blocks_per_tile, KV_BLOCK, d), lambda b, i: (b, i, 0, 0)),
            pl.BlockSpec((None, blocks_per_tile, d, KV_BLOCK), lambda b, i: (b, i, 0, 0)),
        ],
        out_shape=[
            jax.ShapeDtypeStruct((bsz, n_blocks, KV_BLOCK, d), BF16),
            jax.ShapeDtypeStruct((bsz, n_blocks, d, KV_BLOCK), BF16),
        ],
        compiler_params=_params(2),
        name="kv_project",
    )(x, mkv, w_k, w_v_t)


def _bias_body(rb_ref, o_ref):
    rows, tq = o_ref.shape[1], o_ref.shape[2]
    width = 2 * tq
    r0 = pl.program_id(0) * rows
    rb = rb_ref[...]
    hi = rb.astype(BF16)
    rem = rb - hi.astype(F32)
    mid = rem.astype(BF16)
    lo = (rem - mid.astype(F32)).astype(BF16)
    lane = lax.broadcasted_iota(jnp.int32, (1, width), 1)
    rel = lax.broadcasted_iota(jnp.int32, (N_REL, width), 0)
    idx = jnp.clip(lane - (rows - 1) - r0 + LEFT_PAD, -(CHUNK - 1), MAX_REL) + (CHUNK - 1)
    onehot = (rel == idx).astype(BF16)
    wide = (jnp.dot(hi, onehot, preferred_element_type=F32)
            + jnp.dot(mid, onehot, preferred_element_type=F32)
            + jnp.dot(lo, onehot, preferred_element_type=F32)) * LOG2E
    qc = lax.broadcasted_iota(jnp.int32, (1, tq), 1) // CHUNK
    for k in range(rows):
        kc = (r0 + k) // CHUNK
        visible = (kc >= qc) & (kc <= qc + LEFT_CHUNKS)
        o_ref[:, k, :] = jnp.where(visible, wide[:, rows - 1 - k:rows - 1 - k + tq], NEG_BIG)


def _bias_table(rel_bias, tq):
    nh = rel_bias.shape[0]
    tk = tq + LEFT_PAD
    rows = BIAS_ROWS_PER_STEP
    assert tq + rows - 1 <= 2 * tq and tk % rows == 0
    return pl.pallas_call(
        _bias_body,
        grid=(tk // rows,),
        in_specs=[pl.BlockSpec((nh, N_REL), lambda i: (0, 0))],
        out_specs=pl.BlockSpec((nh, rows, tq), lambda i: (0, i, 0)),
        out_shape=jax.ShapeDtypeStruct((nh, tk, tq), F32),
        compiler_params=_params(1),
        name="bias_table",
    )(rel_bias)


def _attn_step(first_tile, x_ref, mod_ref, wq_ref, wo_ref, k_ref, vt_ref, bias_ref, lng_ref, lnb_ref,
               o_ref, ctxt_ref):
    tq, d = x_ref.shape
    n_sub = tq // KV_BLOCK
    win_blocks = (KV_BLOCK + LEFT_PAD) // KV_BLOCK
    tk = win_blocks * KV_BLOCK
    sub_k = ATTN_SUB_Q + LEFT_PAD
    hd = d // N_HEADS
    first_block = (0 if first_tile else pl.program_id(1) * n_sub) - LEFT_PAD // KV_BLOCK
    x = x_ref[...]
    h = _modulate(x, mod_ref, 1)
    q = jnp.dot(h.astype(BF16), wq_ref[...], preferred_element_type=F32) * (hd ** -0.5 * LOG2E)
    q = q.astype(BF16)

    first_head = lax.broadcasted_iota(jnp.int32, (KV_BLOCK, V7X_LANES), 1) < hd
    ones_rows = jnp.ones((V7X_BF16_SUBLANES, KV_BLOCK), BF16)

    def padding_blocks(sub):
        return max(LEFT_PAD - sub * KV_BLOCK, 0) // KV_BLOCK if first_tile else 0

    def scores_t(sub, head):
        lanes = slice(head // 2 * V7X_LANES, (head // 2 + 1) * V7X_LANES)
        q_pair = q[sub * KV_BLOCK:(sub + 1) * KV_BLOCK, lanes]
        skip = padding_blocks(sub)
        k_win = k_ref[pl.ds(first_block + sub + skip, win_blocks - skip), :, lanes]
        k_win = k_win.reshape((win_blocks - skip) * KV_BLOCK, V7X_LANES)
        q_one = jnp.where(first_head if head % 2 == 0 else ~first_head, q_pair, jnp.zeros_like(q_pair))
        return lax.dot_general(k_win, q_one, (((1,), (1,)), ((), ())), preferred_element_type=F32)

    def softmax_t(sub, head, s_t):
        first_row = padding_blocks(sub) * KV_BLOCK
        probs = []
        for r0 in range(0, KV_BLOCK, ATTN_SUB_Q):
            lo = max(r0, first_row)
            s_sub = s_t[lo - first_row:r0 + sub_k - first_row, r0:r0 + ATTN_SUB_Q] + bias_ref[head, lo - r0:, :]
            s_max = jnp.max(s_sub, axis=0, keepdims=True)
            pieces = [jnp.exp2(s_sub - s_max).astype(BF16)]
            if lo - first_row:
                pieces.insert(0, jnp.zeros((lo - first_row, ATTN_SUB_Q), BF16))
            if tk - sub_k - r0:
                pieces.append(jnp.zeros((tk - sub_k - r0, ATTN_SUB_Q), BF16))
            probs.append(jnp.concatenate(pieces, axis=0))
        return jnp.concatenate(probs, axis=1)

    def context_t(sub, head, p_t):
        rows = slice(head * hd, (head + 1) * hd)
        skip = padding_blocks(sub)
        acc = None
        for n in range(skip, win_blocks):
            v_t = vt_ref[pl.ds(first_block + sub + n, 1), rows, :].reshape(hd, KV_BLOCK)
            lhs = jnp.concatenate([v_t, ones_rows], axis=0)
            part = jnp.dot(lhs, p_t[(n - skip) * KV_BLOCK:(n - skip + 1) * KV_BLOCK], preferred_element_type=F32)
            acc = part if acc is None else acc + part
        ctxt_ref[rows, sub * KV_BLOCK:(sub + 1) * KV_BLOCK] = (acc[:hd] / acc[hd:hd + 1]).astype(BF16)

    items = [(sub, head) for sub in range(n_sub) for head in range(N_HEADS)]
    pending = [scores_t(*item) for item in items[:ATTN_HEADS_AHEAD]]
    for n, item in enumerate(items):
        s_cur = pending.pop(0)
        if n + ATTN_HEADS_AHEAD < len(items):
            pending.append(scores_t(*items[n + ATTN_HEADS_AHEAD]))
        context_t(*item, softmax_t(*item, s_cur))

    y = lax.dot_general(ctxt_ref[...], wo_ref[...], (((0,), (0,)), ((), ())), preferred_element_type=F32)
    r = ALPHA * x + _gate(mod_ref, 1) * y
    o_ref[...] = _layer_norm(r, lng_ref[1:2, :], lnb_ref[1:2, :])


def _attn_body(x_ref, *refs):
    first_tile = pl.program_id(1) == 0
    pl.when(first_tile)(functools.partial(_attn_step, True, x_ref, *refs))
    pl.when(jnp.logical_not(first_tile))(functools.partial(_attn_step, False, x_ref, *refs))


def _attention(x, mod, w_q, w_o, k_blocks, vt_blocks, bias, ln_g, ln_b, *, layer, j):
    bsz, seq, d = x.shape
    tq = ATTN_TQ
    assert tq % KV_BLOCK == 0 and tq >= LEFT_PAD
    kv_blocks = k_blocks.shape[1]
    return pl.pallas_call(
        _attn_body,
        grid=(bsz, seq // tq),
        in_specs=[
            pl.BlockSpec((None, tq, d), lambda b, i: (b, i, 0)),
            pl.BlockSpec((None, None, N_MOD, d), lambda b, i: (layer, b, 0, 0)),
            _resident((None, d, d), lambda b, i: (j, 0, 0)),
            _resident((None, d, d), lambda b, i: (j, 0, 0)),
            pl.BlockSpec((None, kv_blocks, KV_BLOCK, d), lambda b, i: (b, 0, 0, 0)),
            pl.BlockSpec((None, kv_blocks, d, KV_BLOCK), lambda b, i: (b, 0, 0, 0)),
            _resident((N_HEADS, ATTN_SUB_Q + LEFT_PAD, ATTN_SUB_Q), lambda b, i: (j, 0, 0)),
            _resident((None, 3, d), lambda b, i: (layer, 0, 0)),
            _resident((None, 3, d), lambda b, i: (layer, 0, 0)),
        ],
        out_specs=pl.BlockSpec((None, tq, d), lambda b, i: (b, i, 0)),
        out_shape=jax.ShapeDtypeStruct((bsz, seq, d), F32),
        scratch_shapes=[pltpu.VMEM((d, tq), BF16)],
        compiler_params=_params(2),
        name="attention",
    )(x, mod, w_q, w_o, k_blocks, vt_blocks, bias, ln_g, ln_b)


def kernel(x, c, w_ada, b_ada, ln_g, ln_b, ffn_gu, ffn_down, gmlp_w_in, gmlp_b_in, gmlp_ln_g, gmlp_ln_b,
           gmlp_w_s, gmlp_b_s, gmlp_w_out, w_ada_kv, b_ada_kv, w_kv, attn_w_q, attn_rel_bias, attn_w_o):
    bsz, seq, d = x.shape
    mod = _adaln(c, w_ada, b_ada, ADALN_TN).reshape(DEPTH, bsz, N_MOD, d)
    mkv = _adaln(c, w_ada_kv[None], b_ada_kv[None], w_ada_kv.shape[1]).reshape(bsz, 2, d)

    w_in_b = gmlp_w_in.astype(BF16)
    w_out_b = gmlp_w_out.astype(BF16)
    w_k_b = w_kv[:, :d].astype(BF16)
    w_v_t_b = w_kv[:, d:].T.astype(BF16)
    w_q_b = attn_w_q.astype(BF16)
    w_o_b = attn_w_o.astype(BF16)
    b_in = gmlp_b_in[:, None, :]
    gln_g = gmlp_ln_g[:, None, :]
    gln_b = gmlp_ln_b[:, None, :]
    b_s_t = jnp.swapaxes(gmlp_b_s, 1, 2)
    bias = _bias_table(attn_rel_bias.reshape(N_B * N_HEADS, N_REL), ATTN_SUB_Q)

    xf = x.reshape(bsz * seq, d)
    k_blocks = vt_blocks = None
    for l in range(DEPTH):
        ffn = functools.partial(_ffn, mod=mod, w_gu=ffn_gu, w_down=ffn_down, ln_g=ln_g, ln_b=ln_b,
                                layer=l, seq=seq, tm=FFN_TM)
        xf = ffn(xf, sub=0, half=0)
        if l < N_A:
            xf = _gmlp(xf, mod, w_in_b, b_in, gln_g, gln_b, gmlp_w_s, b_s_t, w_out_b, ln_g, ln_b,
                       layer=l, seq=seq, tm=GMLP_TM)
        else:
            xf = _attention(xf.reshape(bsz, seq, d), mod, w_q_b, w_o_b, k_blocks, vt_blocks, bias, ln_g, ln_b,
                            layer=l, j=l - N_A).reshape(bsz * seq, d)
        xf = ffn(xf, sub=2, half=1)
        if l == N_A - 1:
            k_blocks, vt_blocks = _kv_project(xf.reshape(bsz, seq, d), mkv, w_k_b, w_v_t_b, KV_TM)
    return xf.reshape(bsz, seq, d)
```

```python
import functools

import jax
import jax.numpy as jnp
from jax import lax
from jax.experimental import pallas as pl
from jax.experimental.pallas import tpu as pltpu

DEPTH = 4
CHUNK = 64
N_A = DEPTH // 2
N_B = DEPTH - N_A
GMLP_WINDOW = 128
GMLP_GROUPS = 8
N_HEADS = 16
LEFT_CHUNKS = 8
LEFT_PAD = LEFT_CHUNKS * CHUNK
MAX_REL = 4 * CHUNK
N_REL = (CHUNK - 1) + MAX_REL + 1
ALPHA = (2.0 * DEPTH) ** 0.25
LN_EPS = 1e-5
N_MOD = 9

V7X_LANES = 128
V7X_MXU_N = 256
V7X_BF16_SUBLANES = 16
V7X_VMEM_LIMIT_BYTES = 56 * 1024 * 1024

NEG_BIG = -1e30
LOG2E = 1.4426950408889634

KV_BLOCK = 256
ATTN_TQ = 512
ATTN_SUB_Q = 128
ATTN_HEADS_AHEAD = 2

FFN_TM = 512
FFN_FINISH_BLOCKS = 8
FFN_FIRST_FINISH_CHUNK = 1
FFN_WEIGHT_SLOTS = 3
GMLP_TM = 512
KV_TM = 1024
GMLP_DOTS_AHEAD = 2
ADALN_TN = 3072
BIAS_ROWS_PER_STEP = 64

F32 = jnp.float32
BF16 = jnp.bfloat16


def _params(n_axes):
    return pltpu.CompilerParams(
        dimension_semantics=("arbitrary",) * n_axes,
        vmem_limit_bytes=V7X_VMEM_LIMIT_BYTES,
    )


def _resident(block_shape, index_map):
    return pl.BlockSpec(block_shape, index_map, pipeline_mode=pl.Buffered(1))


def _layer_norm(r, g, b):
    mu = jnp.mean(r, axis=-1, keepdims=True)
    d = r - mu
    var = jnp.mean(d * d, axis=-1, keepdims=True)
    return d * lax.rsqrt(var + LN_EPS) * g + b


def _gelu(z):
    return 0.5 * z * (1.0 + lax.erf(z * (0.5 ** 0.5)))


def _modulate(x, mod_ref, sub):
    shift = mod_ref[3 * sub:3 * sub + 1, :]
    scale = mod_ref[3 * sub + 1:3 * sub + 2, :]
    return x * (1.0 + scale) + shift


def _gate(mod_ref, sub):
    return 1.0 + mod_ref[3 * sub + 2:3 * sub + 3, :]


def _zero_bits_of(v):
    bits = lax.bitcast_convert_type(v, jnp.int32)
    sixteen = jnp.full(bits.shape, 16, jnp.int32)
    cleared = lax.shift_right_logical(lax.shift_right_logical(bits, sixteen), sixteen)
    return jnp.max(cleared, axis=0, keepdims=True)


def _after(v, finished_rows):
    zero = _zero_bits_of(finished_rows)[:, :v.shape[1]]
    return jnp.where(zero == 0, v, jnp.zeros_like(v))


def _finish_rows(sub, n_blocks, lng_ref, lnb_ref, r_ref, o_ref, k):
    n = o_ref.shape[0] // n_blocks
    rows = slice(k * n, (k + 1) * n)
    out = _layer_norm(r_ref[rows, :], lng_ref[sub:sub + 1, :], lnb_ref[sub:sub + 1, :])
    o_ref[rows, :] = out
    return out


def _pipelined_specs(n_tiles, tiles_per_seq, tm, d, layer):
    cur = lambda i: jnp.minimum(i, n_tiles - 1)
    prev = lambda i: jnp.maximum(i - 1, 0)
    in_specs = [
        pl.BlockSpec((tm, d), lambda i: (cur(i), 0)),
        pl.BlockSpec((None, None, N_MOD, d), lambda i: (layer, cur(i) // tiles_per_seq, 0, 0)),
    ]
    out_spec = pl.BlockSpec((tm, d), lambda i: (prev(i), 0))
    parking = [pltpu.VMEM((tm, d), F32)]
    return in_specs, out_spec, parking


def _adaln_body(c_ref, w_ref, b_ref, o_ref):
    c = c_ref[...]
    c_act = c * jax.nn.sigmoid(c)
    o_ref[...] = jnp.dot(c_act, w_ref[...], preferred_element_type=F32) + b_ref[...]


def _adaln(c, w, b, tn):
    n_l, d, n = w.shape
    bsz = c.shape[0]
    return pl.pallas_call(
        _adaln_body,
        grid=(n_l, n // tn),
        in_specs=[
            pl.BlockSpec((bsz, d), lambda l, j: (0, 0)),
            pl.BlockSpec((None, d, tn), lambda l, j: (l, 0, j)),
            pl.BlockSpec((None, 1, tn), lambda l, j: (l, 0, j)),
        ],
        out_specs=pl.BlockSpec((None, bsz, tn), lambda l, j: (l, 0, j)),
        out_shape=jax.ShapeDtypeStruct((n_l, bsz, n), F32),
        compiler_params=_params(2),
        name="adaln",
    )(c, w, b.reshape(n_l, 1, n))


def _ffn_weight_copies(layer, half, d_ff, wgu_hbm, wd_hbm, stage_gu_ref, stage_d_ref, sem_ref, j):
    lo = j * V7X_MXU_N
    slot = j % FFN_WEIGHT_SLOTS
    return (
        pltpu.make_async_copy(wgu_hbm.at[layer, half, :, pl.ds(lo, V7X_MXU_N)],
                              stage_gu_ref.at[slot, 0], sem_ref.at[slot, 0]),
        pltpu.make_async_copy(wgu_hbm.at[layer, half, :, pl.ds(d_ff + lo, V7X_MXU_N)],
                              stage_gu_ref.at[slot, 1], sem_ref.at[slot, 1]),
        pltpu.make_async_copy(wd_hbm.at[layer, half, pl.ds(lo, V7X_MXU_N), :],
                              stage_d_ref.at[slot], sem_ref.at[slot, 2]),
    )


def _ffn_take_chunk(copies, n_chunks, d_ff, wgu_ref, wd_ref, stage_gu_ref, stage_d_ref, j):
    for copy in copies(j):
        copy.wait()
    lo = j * V7X_MXU_N
    slot = j % FFN_WEIGHT_SLOTS
    wgu_ref[:, lo:lo + V7X_MXU_N] = stage_gu_ref[slot, 0].astype(BF16)
    wgu_ref[:, d_ff + lo:d_ff + lo + V7X_MXU_N] = stage_gu_ref[slot, 1].astype(BF16)
    wd_ref[lo:lo + V7X_MXU_N, :] = stage_d_ref[slot].astype(BF16)
    if j + FFN_WEIGHT_SLOTS < n_chunks:
        for copy in copies(j + FFN_WEIGHT_SLOTS):
            copy.start()


def _ffn_start(sub, d_ff, x_ref, mod_ref, wgu_ref, wd_ref, a_ref, r_ref, finish_rows=None, take_chunk=None):
    x = x_ref[...]
    h = _modulate(x, mod_ref, sub).astype(BF16)
    for j in range(d_ff // V7X_MXU_N):
        lo = j * V7X_MXU_N
        if take_chunk is not None:
            take_chunk(j)
        g = jnp.dot(h, wgu_ref[:, lo:lo + V7X_MXU_N], preferred_element_type=F32)
        u = jnp.dot(h, wgu_ref[:, d_ff + lo:d_ff + lo + V7X_MXU_N], preferred_element_type=F32)
        act = g * jax.nn.sigmoid(g) * u
        k = j - FFN_FIRST_FINISH_CHUNK
        if finish_rows is not None and 0 <= k < FFN_FINISH_BLOCKS:
            act = _after(act, finish_rows(k))
        a_ref[:, lo:lo + V7X_MXU_N] = act.astype(BF16)
    y = jnp.dot(a_ref[...], wd_ref[...], preferred_element_type=F32)
    r_ref[...] = ALPHA * x + (0.5 * _gate(mod_ref, sub)) * y


def _ffn_body(layer, half, sub, d_ff, x_ref, mod_ref, wgu_hbm, wd_hbm, lng_ref, lnb_ref, o_ref,
              wgu_ref, wd_ref, stage_gu_ref, stage_d_ref, sem_ref, a_ref, r_ref):
    i = pl.program_id(0)
    last = pl.num_programs(0) - 1
    n_chunks = d_ff // V7X_MXU_N
    start = functools.partial(_ffn_start, sub, d_ff, x_ref, mod_ref, wgu_ref, wd_ref, a_ref, r_ref)
    finish_rows = functools.partial(_finish_rows, sub, FFN_FINISH_BLOCKS, lng_ref, lnb_ref, r_ref, o_ref)
    copies = functools.partial(_ffn_weight_copies, layer, half, d_ff, wgu_hbm, wd_hbm, stage_gu_ref, stage_d_ref,
                               sem_ref)
    take_chunk = functools.partial(_ffn_take_chunk, copies, n_chunks, d_ff, wgu_ref, wd_ref, stage_gu_ref,
                                   stage_d_ref)

    @pl.when(i == 0)
    def _():
        for j in range(FFN_WEIGHT_SLOTS):
            for copy in copies(j):
                copy.start()
        start(take_chunk=take_chunk)

    @pl.when((i > 0) & (i < last))
    def _():
        start(finish_rows=finish_rows)

    @pl.when(i == last)
    def _():
        for k in range(FFN_FINISH_BLOCKS):
            finish_rows(k)


def _ffn(x, mod, w_gu, w_down, ln_g, ln_b, *, layer, sub, half, seq, tm):
    m, d = x.shape
    d_ff = w_down.shape[2]
    n_tiles = m // tm
    in_specs, out_spec, parking = _pipelined_specs(n_tiles, seq // tm, tm, d, layer)
    return pl.pallas_call(
        functools.partial(_ffn_body, layer, half, sub, d_ff),
        grid=(n_tiles + 1,),
        in_specs=in_specs + [
            pl.BlockSpec(memory_space=pl.ANY),
            pl.BlockSpec(memory_space=pl.ANY),
            _resident((None, 3, d), lambda i: (layer, 0, 0)),
            _resident((None, 3, d), lambda i: (layer, 0, 0)),
        ],
        out_specs=out_spec,
        out_shape=jax.ShapeDtypeStruct((m, d), F32),
        scratch_shapes=[
            pltpu.VMEM((d, 2 * d_ff), BF16),
            pltpu.VMEM((d_ff, d), BF16),
            pltpu.VMEM((FFN_WEIGHT_SLOTS, 2, d, V7X_MXU_N), F32),
            pltpu.VMEM((FFN_WEIGHT_SLOTS, V7X_MXU_N, d), F32),
            pltpu.SemaphoreType.DMA((FFN_WEIGHT_SLOTS, 3)),
            pltpu.VMEM((tm, d_ff), BF16),
        ] + parking,
        compiler_params=_params(1),
        name="ffn",
    )(x, mod, w_gu, w_down, ln_g, ln_b)


def _gmlp_body(x_ref, mod_ref, win_ref, bin_ref, glng_ref, glnb_ref, ws_ref, bst_ref, wout_ref,
               lng_ref, lnb_ref, o_ref, v_ref, m_ref):
    tm = x_ref.shape[0]
    half = wout_ref.shape[0]
    gdim = half // GMLP_GROUPS
    x = x_ref[...]
    h = _modulate(x, mod_ref, 1).astype(BF16)

    def project(lo):
        return jnp.dot(h, win_ref[:, lo:lo + gdim], preferred_element_type=F32) + bin_ref[:, lo:lo + gdim]

    order = [half + g * gdim for g in range(GMLP_GROUPS)] + [g * gdim for g in range(GMLP_GROUPS)]
    pending = [project(lo) for lo in order[:GMLP_DOTS_AHEAD]]

    def next_projection(n):
        z = pending.pop(0)
        if n + GMLP_DOTS_AHEAD < len(order):
            pending.append(project(order[n + GMLP_DOTS_AHEAD]))
        return z

    for g in range(GMLP_GROUPS):
        v_ref[:, g * gdim:(g + 1) * gdim] = _gelu(next_projection(g))
    v = v_ref[...]
    mu = jnp.mean(v, axis=-1, keepdims=True)
    dv = v - mu
    rstd = lax.rsqrt(jnp.mean(dv * dv, axis=-1, keepdims=True) + LN_EPS)

    t_out = lax.broadcasted_iota(jnp.int32, (GMLP_WINDOW, GMLP_WINDOW), 0)
    s_in = lax.broadcasted_iota(jnp.int32, (GMLP_WINDOW, GMLP_WINDOW), 1)
    causal = (s_in // CHUNK) <= (t_out // CHUNK)

    for g in range(GMLP_GROUPS):
        lo = g * gdim
        ws_g = jnp.where(causal, ws_ref[g], 0.0).astype(BF16)
        b_col = bst_ref[:, g:g + 1]
        z_u = next_projection(GMLP_GROUPS + g)
        for w in range(tm // GMLP_WINDOW):
            rows = slice(w * GMLP_WINDOW, (w + 1) * GMLP_WINDOW)
            vn = ((v_ref[rows, lo:lo + gdim] - mu[rows]) * rstd[rows] * glng_ref[:, lo:lo + gdim]
                  + glnb_ref[:, lo:lo + gdim]).astype(BF16)
            s = jnp.dot(ws_g, vn, preferred_element_type=F32) + b_col
            m_ref[rows, lo:lo + gdim] = (_gelu(z_u[rows]) * s).astype(BF16)

    y = jnp.dot(m_ref[...], wout_ref[...], preferred_element_type=F32)
    r = ALPHA * x_ref[...] + _gate(mod_ref, 1) * y
    o_ref[...] = _layer_norm(r, lng_ref[1:2, :], lnb_ref[1:2, :])


def _gmlp(x, mod, w_in, b_in, gln_g, gln_b, w_s, b_s_t, w_out, ln_g, ln_b, *, layer, seq, tm):
    m, d = x.shape
    width = w_in.shape[2]
    half = width // 2
    tiles_per_seq = seq // tm
    const = lambda i: (layer, 0, 0)
    return pl.pallas_call(
        _gmlp_body,
        grid=(m // tm,),
        in_specs=[
            pl.BlockSpec((tm, d), lambda i: (i, 0)),
            pl.BlockSpec((None, None, N_MOD, d), lambda i: (layer, i // tiles_per_seq, 0, 0)),
            _resident((None, d, width), const),
            _resident((None, 1, width), const),
            _resident((None, 1, half), const),
            _resident((None, 1, half), const),
            _resident((None, GMLP_GROUPS, GMLP_WINDOW, GMLP_WINDOW), lambda i: (layer, 0, 0, 0)),
            _resident((None, GMLP_WINDOW, GMLP_GROUPS), const),
            _resident((None, half, d), const),
            _resident((None, 3, d), const),
            _resident((None, 3, d), const),
        ],
        out_specs=pl.BlockSpec((tm, d), lambda i: (i, 0)),
        out_shape=jax.ShapeDtypeStruct((m, d), F32),
        scratch_shapes=[pltpu.VMEM((tm, half), F32), pltpu.VMEM((tm, half), BF16)],
        compiler_params=_params(1),
        name="gmlp",
    )(x, mod, w_in, b_in, gln_g, gln_b, w_s, b_s_t, w_out, ln_g, ln_b)


def _kv_body(x_ref, mkv_ref, wk_ref, wvt_ref, k_ref, vt_ref):
    h = (x_ref[...] * (1.0 + mkv_ref[1:2, :]) + mkv_ref[0:1, :]).astype(BF16)
    k = jnp.dot(h, wk_ref[...], preferred_element_type=F32)
    k_ref[...] = k.astype(BF16).reshape(k_ref.shape)
    vt = lax.dot_general(wvt_ref[...], h, (((1,), (1,)), ((), ())), preferred_element_type=F32)
    for n in range(vt_ref.shape[0]):
        vt_ref[n] = vt[:, n * KV_BLOCK:(n + 1) * KV_BLOCK].astype(BF16)


def _kv_project(x, mkv, w_k, w_v_t, tm):
    bsz, seq, d = x.shape
    blocks_per_tile = tm // KV_BLOCK
    n_blocks = seq // KV_BLOCK
    return pl.pallas_call(
        _kv_body,
        grid=(bsz, seq // tm),
        in_specs=[
            pl.BlockSpec((None, tm, d), lambda b, i: (b, i, 0)),
            pl.BlockSpec((None, 2, d), lambda b, i: (b, 0, 0)),
            _resident((d, d), lambda b, i: (0, 0)),
            _resident((d, d), lambda b, i: (0, 0)),
        ],
        out_specs=[
            pl.BlockSpec((None, blocks_per_tile, KV_BLOCK, d), lambda b, i: (b, i, 0, 0)),
            pl.BlockSpec((None, blocks_per_tile, d, KV_BLOCK), lambda b, i: (b, i, 0, 0)),
        ],
        out_shape=[
            jax.ShapeDtypeStruct((bsz, n_blocks, KV_BLOCK, d), BF16),
            jax.ShapeDtypeStruct((bsz, n_blocks, d, KV_BLOCK), BF16),
        ],
        compiler_params=_params(2),
        name="kv_project",
    )(x, mkv, w_k, w_v_t)


def _bias_body(rb_ref, o_ref):
    rows, tq = o_ref.shape[1], o_ref.shape[2]
    width = 2 * tq
    r0 = pl.program_id(0) * rows
    rb = rb_ref[...]
    hi = rb.astype(BF16)
    rem = rb - hi.astype(F32)
    mid = rem.astype(BF16)
    lo = (rem - mid.astype(F32)).astype(BF16)
    lane = lax.broadcasted_iota(jnp.int32, (1, width), 1)
    rel = lax.broadcasted_iota(jnp.int32, (N_REL, width), 0)
    idx = jnp.clip(lane - (rows - 1) - r0 + LEFT_PAD, -(CHUNK - 1), MAX_REL) + (CHUNK - 1)
    onehot = (rel == idx).astype(BF16)
    wide = (jnp.dot(hi, onehot, preferred_element_type=F32)
            + jnp.dot(mid, onehot, preferred_element_type=F32)
            + jnp.dot(lo, onehot, preferred_element_type=F32)) * LOG2E
    qc = lax.broadcasted_iota(jnp.int32, (1, tq), 1) // CHUNK
    for k in range(rows):
        kc = (r0 + k) // CHUNK
        visible = (kc >= qc) & (kc <= qc + LEFT_CHUNKS)
        o_ref[:, k, :] = jnp.where(visible, wide[:, rows - 1 - k:rows - 1 - k + tq], NEG_BIG)


def _bias_table(rel_bias, tq):
    nh = rel_bias.shape[0]
    tk = tq + LEFT_PAD
    rows = BIAS_ROWS_PER_STEP
    assert tq + rows - 1 <= 2 * tq and tk % rows == 0
    return pl.pallas_call(
        _bias_body,
        grid=(tk // rows,),
        in_specs=[pl.BlockSpec((nh, N_REL), lambda i: (0, 0))],
        out_specs=pl.BlockSpec((nh, rows, tq), lambda i: (0, i, 0)),
        out_shape=jax.ShapeDtypeStruct((nh, tk, tq), F32),
        compiler_params=_params(1),
        name="bias_table",
    )(rel_bias)


def _attn_step(first_tile, x_ref, mod_ref, wq_ref, wo_ref, k_ref, vt_ref, bias_ref, lng_ref, lnb_ref,
               o_ref, ctxt_ref):
    tq, d = x_ref.shape
    n_sub = tq // KV_BLOCK
    win_blocks = (KV_BLOCK + LEFT_PAD) // KV_BLOCK
    tk = win_blocks * KV_BLOCK
    sub_k = ATTN_SUB_Q + LEFT_PAD
    hd = d // N_HEADS
    first_block = (0 if first_tile else pl.program_id(1) * n_sub) - LEFT_PAD // KV_BLOCK
    x = x_ref[...]
    h = _modulate(x, mod_ref, 1)
    q = jnp.dot(h.astype(BF16), wq_ref[...], preferred_element_type=F32) * (hd ** -0.5 * LOG2E)
    q = q.astype(BF16)

    first_head = lax.broadcasted_iota(jnp.int32, (KV_BLOCK, V7X_LANES), 1) < hd
    ones_rows = jnp.ones((V7X_BF16_SUBLANES, KV_BLOCK), BF16)

    def padding_blocks(sub):
        return max(LEFT_PAD - sub * KV_BLOCK, 0) // KV_BLOCK if first_tile else 0

    def scores_t(sub, head):
        lanes = slice(head // 2 * V7X_LANES, (head // 2 + 1) * V7X_LANES)
        q_pair = q[sub * KV_BLOCK:(sub + 1) * KV_BLOCK, lanes]
        skip = padding_blocks(sub)
        k_win = k_ref[pl.ds(first_block + sub + skip, win_blocks - skip), :, lanes]
        k_win = k_win.reshape((win_blocks - skip) * KV_BLOCK, V7X_LANES)
        q_one = jnp.where(first_head if head % 2 == 0 else ~first_head, q_pair, jnp.zeros_like(q_pair))
        return lax.dot_general(k_win, q_one, (((1,), (1,)), ((), ())), preferred_element_type=F32)

    def softmax_t(sub, head, s_t):
        first_row = padding_blocks(sub) * KV_BLOCK
        probs = []
        for r0 in range(0, KV_BLOCK, ATTN_SUB_Q):
            lo = max(r0, first_row)
            s_sub = s_t[lo - first_row:r0 + sub_k - first_row, r0:r0 + ATTN_SUB_Q] + bias_ref[head, lo - r0:, :]
            s_max = jnp.max(s_sub, axis=0, keepdims=True)
            pieces = [jnp.exp2(s_sub - s_max).astype(BF16)]
            if lo - first_row:
                pieces.insert(0, jnp.zeros((lo - first_row, ATTN_SUB_Q), BF16))
            if tk - sub_k - r0:
                pieces.append(jnp.zeros((tk - sub_k - r0, ATTN_SUB_Q), BF16))
            probs.append(jnp.concatenate(pieces, axis=0))
        return jnp.concatenate(probs, axis=1)

    def context_t(sub, head, p_t):
        rows = slice(head * hd, (head + 1) * hd)
        skip = padding_blocks(sub)
        acc = None
        for n in range(skip, win_blocks):
            v_t = vt_ref[pl.ds(first_block + sub + n, 1), rows, :].reshape(hd, KV_BLOCK)
            lhs = jnp.concatenate([v_t, ones_rows], axis=0)
            part = jnp.dot(lhs, p_t[(n - skip) * KV_BLOCK:(n - skip + 1) * KV_BLOCK], preferred_element_type=F32)
            acc = part if acc is None else acc + part
        ctxt_ref[rows, sub * KV_BLOCK:(sub + 1) * KV_BLOCK] = (acc[:hd] / acc[hd:hd + 1]).astype(BF16)

    items = [(sub, head) for sub in range(n_sub) for head in range(N_HEADS)]
    pending = [scores_t(*item) for item in items[:ATTN_HEADS_AHEAD]]
    for n, item in enumerate(items):
        s_cur = pending.pop(0)
        if n + ATTN_HEADS_AHEAD < len(items):
            pending.append(scores_t(*items[n + ATTN_HEADS_AHEAD]))
        context_t(*item, softmax_t(*item, s_cur))

    y = lax.dot_general(ctxt_ref[...], wo_ref[...], (((0,), (0,)), ((), ())), preferred_element_type=F32)
    r = ALPHA * x + _gate(mod_ref, 1) * y
    o_ref[...] = _layer_norm(r, lng_ref[1:2, :], lnb_ref[1:2, :])


def _attn_body(x_ref, *refs):
    first_tile = pl.program_id(1) == 0
    pl.when(first_tile)(functools.partial(_attn_step, True, x_ref, *refs))
    pl.when(jnp.logical_not(first_tile))(functools.partial(_attn_step, False, x_ref, *refs))


def _attention(x, mod, w_q, w_o, k_blocks, vt_blocks, bias, ln_g, ln_b, *, layer, j):
    bsz, seq, d = x.shape
    tq = ATTN_TQ
    assert tq % KV_BLOCK == 0 and tq >= LEFT_PAD
    kv_blocks = k_blocks.shape[1]
    return pl.pallas_call(
        _attn_body,
        grid=(bsz, seq // tq),
        in_specs=[
            pl.BlockSpec((None, tq, d), lambda b, i: (b, i, 0)),
            pl.BlockSpec((None, None, N_MOD, d), lambda b, i: (layer, b, 0, 0)),
            _resident((None, d, d), lambda b, i: (j, 0, 0)),
            _resident((None, d, d), lambda b, i: (j, 0, 0)),
            pl.BlockSpec((None, kv_blocks, KV_BLOCK, d), lambda b, i: (b, 0, 0, 0)),
            pl.BlockSpec((None, kv_blocks, d, KV_BLOCK), lambda b, i: (b, 0, 0, 0)),
            _resident((N_HEADS, ATTN_SUB_Q + LEFT_PAD, ATTN_SUB_Q), lambda b, i: (j, 0, 0)),
            _resident((None, 3, d), lambda b, i: (layer, 0, 0)),
            _resident((None, 3, d), lambda b, i: (layer, 0, 0)),
        ],
        out_specs=pl.BlockSpec((None, tq, d), lambda b, i: (b, i, 0)),
        out_shape=jax.ShapeDtypeStruct((bsz, seq, d), F32),
        scratch_shapes=[pltpu.VMEM((d, tq), BF16)],
        compiler_params=_params(2),
        name="attention",
    )(x, mod, w_q, w_o, k_blocks, vt_blocks, bias, ln_g, ln_b)


def kernel(x, c, w_ada, b_ada, ln_g, ln_b, ffn_gu, ffn_down, gmlp_w_in, gmlp_b_in, gmlp_ln_g, gmlp_ln_b,
           gmlp_w_s, gmlp_b_s, gmlp_w_out, w_ada_kv, b_ada_kv, w_kv, attn_w_q, attn_rel_bias, attn_w_o):
    bsz, seq, d = x.shape
    mod = _adaln(c, w_ada, b_ada, ADALN_TN).reshape(DEPTH, bsz, N_MOD, d)
    mkv = _adaln(c, w_ada_kv[None], b_ada_kv[None], w_ada_kv.shape[1]).reshape(bsz, 2, d)

    w_in_b = gmlp_w_in.astype(BF16)
    w_out_b = gmlp_w_out.astype(BF16)
    w_k_b = w_kv[:, :d].astype(BF16)
    w_v_t_b = w_kv[:, d:].T.astype(BF16)
    w_q_b = attn_w_q.astype(BF16)
    w_o_b = attn_w_o.astype(BF16)
    b_in = gmlp_b_in[:, None, :]
    gln_g = gmlp_ln_g[:, None, :]
    gln_b = gmlp_ln_b[:, None, :]
    b_s_t = jnp.swapaxes(gmlp_b_s, 1, 2)
    bias = _bias_table(attn_rel_bias.reshape(N_B * N_HEADS, N_REL), ATTN_SUB_Q)

    xf = x.reshape(bsz * seq, d)
    k_blocks = vt_blocks = None
    for l in range(DEPTH):
        ffn = functools.partial(_ffn, mod=mod, w_gu=ffn_gu, w_down=ffn_down, ln_g=ln_g, ln_b=ln_b,
                                layer=l, seq=seq, tm=FFN_TM)
        xf = ffn(xf, sub=0, half=0)
        if l < N_A:
            xf = _gmlp(xf, mod, w_in_b, b_in, gln_g, gln_b, gmlp_w_s, b_s_t, w_out_b, ln_g, ln_b,
                       layer=l, seq=seq, tm=GMLP_TM)
        else:
            xf = _attention(xf.reshape(bsz, seq, d), mod, w_q_b, w_o_b, k_blocks, vt_blocks, bias, ln_g, ln_b,
                            layer=l, j=l - N_A).reshape(bsz * seq, d)
        xf = ffn(xf, sub=2, half=1)
        if l == N_A - 1:
            k_blocks, vt_blocks = _kv_project(xf.reshape(bsz, seq, d), mkv, w_k_b, w_v_t_b, KV_TM)
    return xf.reshape(bsz, seq, d)
```
